```python
import math
import jax, jax.numpy as jnp
from jax import lax
import numpy as np

D_MODEL = 1024
BATCH = 8
SEQ = 2048
DEPTH = 1
DEC_BATCH = 128
DEC_SEQ = 4
PAST_LEN = 16384
PAGE_SIZE = 128

D_MIX = 2 * D_MODEL
D_POOL = D_MIX // 2
POOL_WINDOWS = (2, 4, 8, 16)
N_POOL_GROUPS = len(POOL_WINDOWS)
POOL_GROUP = D_POOL // N_POOL_GROUPS
POOL_BUF = max(POOL_WINDOWS) - 1
D_SSM = D_MIX - D_POOL
SSM_HEAD_DIM = 64
SSM_HEADS = D_SSM // SSM_HEAD_DIM
SSM_GROUPS = 2
HEADS_PER_GROUP = SSM_HEADS // SSM_GROUPS
D_STATE = 128
CONV_WIDTH = 4
CONV_DIM = D_SSM + 2 * SSM_GROUPS * D_STATE
CHUNK = 128
D_FF = 4 * D_MODEL
D_PLE = 256
D_IN_PROJ = D_POOL + D_SSM + CONV_DIM + SSM_HEADS
EPS = 1e-6

kernel_name = 'hymba_pool_ssd_decoder_step'


def _rmsnorm(x, g):
    xf = x.astype(jnp.float32)
    y = xf * lax.rsqrt(jnp.mean(xf * xf, axis=-1, keepdims=True) + EPS)
    return (y * g.astype(jnp.float32)).astype(x.dtype)


def _pool_mixer(u, buf, pos, w_pool, pool_scale):
    T = u.shape[1]
    ext = jnp.concatenate([buf.astype(u.dtype), u], axis=1)
    ext32 = ext.astype(jnp.float32)
    cs = jnp.cumsum(ext32, axis=1)
    cs = jnp.concatenate([jnp.zeros_like(cs[:, :1]), cs], axis=1)
    outs = []
    for g, w in enumerate(POOL_WINDOWS):
        sl = slice(g * POOL_GROUP, (g + 1) * POOL_GROUP)
        win_sum = (cs[:, POOL_BUF + 1:POOL_BUF + 1 + T, sl]
                   - cs[:, POOL_BUF + 1 - w:POOL_BUF + 1 - w + T, sl])
        count = jnp.minimum(w, pos + 1).astype(jnp.float32)[None, :, None]
        d = win_sum / count - ext32[:, POOL_BUF:, sl]
        outs.append(jnp.einsum('btc,ce->bte', d, w_pool[g].astype(jnp.float32)))
    y = jnp.concatenate(outs, axis=-1) * pool_scale.astype(jnp.float32)
    return y.astype(u.dtype), ext[:, -POOL_BUF:]


def _causal_conv(xbc, buf, conv_w, conv_b):
    T = xbc.shape[1]
    ext = jnp.concatenate([buf.astype(xbc.dtype), xbc], axis=1)
    acc = conv_b
    for k in range(CONV_WIDTH):
        acc = acc + ext[:, k:k + T] * conv_w[k]
    return jax.nn.silu(acc), ext[:, -(CONV_WIDTH - 1):]


def _ssd_scan(x, dt, A, Bm, Cm, h0):
    b, T = x.shape[:2]
    L = min(CHUNK, T)
    nc = -(-T // L)
    pad = nc * L - T
    if pad:
        x = jnp.pad(x, ((0, 0), (0, pad), (0, 0), (0, 0)))
        dt = jnp.pad(dt, ((0, 0), (0, pad), (0, 0)))
        Bm = jnp.pad(Bm, ((0, 0), (0, pad), (0, 0), (0, 0)))
        Cm = jnp.pad(Cm, ((0, 0), (0, pad), (0, 0), (0, 0)))
    G, Hg = SSM_GROUPS, HEADS_PER_GROUP
    xc = x.reshape(b, nc, L, G, Hg, SSM_HEAD_DIM)
    dtc = dt.reshape(b, nc, L, G, Hg)
    Bc = Bm.reshape(b, nc, L, G, D_STATE)
    Cc = Cm.reshape(b, nc, L, G, D_STATE)
    a_cum = jnp.cumsum(dtc * A.reshape(G, Hg), axis=2)
    seg = a_cum[:, :, :, None] - a_cum[:, :, None, :]
    causal = jnp.tril(jnp.ones((L, L), dtype=bool))[None, None, :, :, None, None]
    decay = jnp.exp(jnp.where(causal, seg, -jnp.inf))
    cb = jnp.einsum('bclgn,bcsgn->bclsg', Cc, Bc)
    y_diag = jnp.einsum('bclsg,bclsgh,bcsgh,bcsghp->bclghp', cb, decay, dtc, xc)
    decay_end = jnp.exp(a_cum[:, :, -1:] - a_cum)
    chunk_states = jnp.einsum('bclgn,bclgh,bclghp->bcghpn', Bc, decay_end * dtc, xc)
    chunk_decay = jnp.exp(a_cum[:, :, -1])

    def step(h, inp):
        st, dec = inp
        return h * dec[..., None, None] + st, h

    h_init = h0.reshape(b, G, Hg, SSM_HEAD_DIM, D_STATE)
    h_final, h_prev = lax.scan(step, h_init,
                               (jnp.moveaxis(chunk_states, 1, 0), jnp.moveaxis(chunk_decay, 1, 0)))
    h_prev = jnp.moveaxis(h_prev, 0, 1)
    y_off = jnp.einsum('bclgn,bcghpn,bclgh->bclghp', Cc, h_prev, jnp.exp(a_cum))
    y = (y_diag + y_off).reshape(b, nc * L, SSM_HEADS, SSM_HEAD_DIM)[:, :T]
    return y, h_final.reshape(b, SSM_HEADS, SSM_HEAD_DIM, D_STATE)


def _mixer(xn, pos, pool_buf, conv_buf, ssm_state, w_in, w_pool, pool_scale, conv_w, conv_b,
           dt_bias, a_log, d_skip, ssm_norm_g, w_out):
    b, T = xn.shape[:2]
    proj = jnp.einsum('btd,de->bte', xn, w_in)
    u, z, xbc, dt_raw = jnp.split(proj, [D_POOL, D_POOL + D_SSM, D_POOL + D_SSM + CONV_DIM], axis=-1)
    y_pool, new_pool = _pool_mixer(u, pool_buf, pos, w_pool, pool_scale)
    xbc_c, new_conv = _causal_conv(xbc, conv_buf, conv_w, conv_b)
    xs, Bm, Cm = jnp.split(xbc_c, [D_SSM, D_SSM + SSM_GROUPS * D_STATE], axis=-1)
    x32 = xs.astype(jnp.float32).reshape(b, T, SSM_HEADS, SSM_HEAD_DIM)
    B32 = Bm.astype(jnp.float32).reshape(b, T, SSM_GROUPS, D_STATE)
    C32 = Cm.astype(jnp.float32).reshape(b, T, SSM_GROUPS, D_STATE)
    dt = jax.nn.softplus(dt_raw.astype(jnp.float32) + dt_bias.astype(jnp.float32))
    A = -jnp.exp(a_log.astype(jnp.float32))
    y, new_ssm = _ssd_scan(x32, dt, A, B32, C32, ssm_state.astype(jnp.float32))
    y = y + d_skip.astype(jnp.float32)[:, None] * x32
    y = y.reshape(b, T, D_SSM) * jax.nn.silu(z.astype(jnp.float32))
    y_ssm = _rmsnorm(y, ssm_norm_g).astype(xn.dtype)
    out = jnp.einsum('btm,md->btd', jnp.concatenate([y_pool, y_ssm], axis=-1), w_out)
    return out, new_pool, new_conv, new_ssm.astype(ssm_state.dtype)


def _layer(h, p, pos, pool_buf, conv_buf, ssm_state, w_in, w_pool, pool_scale, conv_w, conv_b,
           dt_bias, a_log, d_skip, ssm_norm_g, w_out, norm_mix_g, norm_mlp_g, w_ff1, w_ff2,
           norm_ple_g, w_gate, w_ple):
    mix, new_pool, new_conv, new_ssm = _mixer(
        _rmsnorm(h, norm_mix_g), pos, pool_buf, conv_buf, ssm_state, w_in, w_pool, pool_scale,
        conv_w, conv_b, dt_bias, a_log, d_skip, ssm_norm_g, w_out)
    h = h + mix
    hn = _rmsnorm(h, norm_mlp_g)
    h = h + jnp.einsum('btf,fd->btd', jnp.square(jax.nn.relu(jnp.einsum('btd,df->btf', hn, w_ff1))), w_ff2)
    gate = jax.nn.sigmoid(jnp.einsum('btd,de->bte', _rmsnorm(h, norm_ple_g), w_gate))
    h = h + gate * jnp.einsum('btk,kd->btd', p, w_ple)
    return h, new_pool, new_conv, new_ssm


def _trunk(x, p, pos, pool_st, conv_st, ssm_st, layer_params, final_norm_g):
    h = x
    pools, convs, ssms = [], [], []
    for i in range(DEPTH):
        lp = [a[i] for a in layer_params]
        h, np_, nc_, ns_ = _layer(h, p[i], pos, pool_st[i], conv_st[i], ssm_st[i], *lp)
        pools.append(np_)
        convs.append(nc_)
        ssms.append(ns_)
    return _rmsnorm(h, final_norm_g), jnp.stack(pools), jnp.stack(convs), jnp.stack(ssms)


def setup_inputs(seed: int = 0) -> dict:
    key = jax.random.key(seed)
    ks = jax.random.split(key, 32)
    f32 = jnp.float32
    nrm = lambda k, shape, s: jax.random.normal(k, shape, f32) * s
    dt_init = jnp.exp(jax.random.uniform(ks[11], (DEPTH, SSM_HEADS), f32, math.log(1e-3), math.log(1e-1)))
    return {
        'x_prompt': nrm(ks[0], (BATCH, SEQ, D_MODEL), 1.0),
        'x_sample': nrm(ks[1], (DEC_BATCH, DEC_SEQ, D_MODEL), 1.0),
        'p_prompt': nrm(ks[2], (DEPTH, BATCH, SEQ, D_PLE), 1.0),
        'p_sample': nrm(ks[3], (DEPTH, DEC_BATCH, DEC_SEQ, D_PLE), 1.0),
        'state_pool': nrm(ks[4], (DEPTH, DEC_BATCH, POOL_BUF, D_POOL), 1.0),
        'state_conv': nrm(ks[5], (DEPTH, DEC_BATCH, CONV_WIDTH - 1, CONV_DIM), 1.0),
        'state_ssm': nrm(ks[6], (DEPTH, DEC_BATCH, SSM_HEADS, SSM_HEAD_DIM, D_STATE), 0.5),
        'w_in': nrm(ks[7], (DEPTH, D_MODEL, D_IN_PROJ), D_MODEL ** -0.5),
        'w_pool': nrm(ks[8], (DEPTH, N_POOL_GROUPS, POOL_GROUP, POOL_GROUP), POOL_GROUP ** -0.5),
        'pool_scale': 1.0 + nrm(ks[9], (DEPTH, D_POOL), 0.1),
        'conv_w': nrm(ks[10], (DEPTH, CONV_WIDTH, CONV_DIM), CONV_WIDTH ** -0.5),
        'conv_b': nrm(ks[12], (DEPTH, CONV_DIM), 0.01),
        'dt_bias': dt_init + jnp.log(-jnp.expm1(-dt_init)),
        'a_log': jnp.log(jax.random.uniform(ks[13], (DEPTH, SSM_HEADS), f32, 1.0, 16.0)),
        'd_skip': 1.0 + nrm(ks[14], (DEPTH, SSM_HEADS), 0.1),
        'ssm_norm_g': 1.0 + nrm(ks[15], (DEPTH, D_SSM), 0.05),
        'w_out': nrm(ks[16], (DEPTH, D_MIX, D_MODEL), D_MIX ** -0.5),
        'norm_mix_g': 1.0 + nrm(ks[17], (DEPTH, D_MODEL), 0.05),
        'norm_mlp_g': 1.0 + nrm(ks[18], (DEPTH, D_MODEL), 0.05),
        'w_ff1': nrm(ks[19], (DEPTH, D_MODEL, D_FF), D_MODEL ** -0.5),
        'w_ff2': nrm(ks[20], (DEPTH, D_FF, D_MODEL), D_FF ** -0.5),
        'norm_ple_g': 1.0 + nrm(ks[21], (DEPTH, D_MODEL), 0.05),
        'w_gate': nrm(ks[22], (DEPTH, D_MODEL, D_MODEL), D_MODEL ** -0.5),
        'w_ple': nrm(ks[23], (DEPTH, D_PLE, D_MODEL), D_PLE ** -0.5),
        'final_norm_g': 1.0 + nrm(ks[24], (D_MODEL,), 0.05),
    }


def reference(x_prompt, x_sample, p_prompt, p_sample, state_pool, state_conv, state_ssm,
              w_in, w_pool, pool_scale, conv_w, conv_b, dt_bias, a_log, d_skip, ssm_norm_g,
              w_out, norm_mix_g, norm_mlp_g, w_ff1, w_ff2, norm_ple_g, w_gate, w_ple, final_norm_g):
    layer_params = (w_in, w_pool, pool_scale, conv_w, conv_b, dt_bias, a_log, d_skip, ssm_norm_g,
                    w_out, norm_mix_g, norm_mlp_g, w_ff1, w_ff2, norm_ple_g, w_gate, w_ple)
    b_p, t_p = x_prompt.shape[:2]
    t_s = x_sample.shape[1]
    zero_pool = jnp.zeros((DEPTH, b_p, POOL_BUF, D_POOL), x_prompt.dtype)
    zero_conv = jnp.zeros((DEPTH, b_p, CONV_WIDTH - 1, CONV_DIM), x_prompt.dtype)
    zero_ssm = jnp.zeros((DEPTH, b_p, SSM_HEADS, SSM_HEAD_DIM, D_STATE), x_prompt.dtype)
    pos_prompt = jnp.arange(t_p, dtype=jnp.int32)
    pos_sample = PAST_LEN + jnp.arange(t_s, dtype=jnp.int32)
    y_prompt, pool_p, conv_p, ssm_p = _trunk(x_prompt, p_prompt, pos_prompt, zero_pool, zero_conv,
                                             zero_ssm, layer_params, final_norm_g)
    y_sample, pool_s, conv_s, ssm_s = _trunk(x_sample, p_sample, pos_sample, state_pool, state_conv,
                                             state_ssm, layer_params, final_norm_g)
    return (y_prompt, y_sample, pool_p, conv_p, ssm_p, pool_s, conv_s, ssm_s)
```

```python
import functools
import math

import jax
import jax.numpy as jnp
from jax import lax
from jax.experimental import pallas as pl
from jax.experimental.pallas import tpu as pltpu

F32 = jnp.float32
BF16 = jnp.bfloat16
HIGHEST = lax.Precision.HIGHEST

D_MODEL = 1024
D_POOL = 1024
D_SSM = 1024
POOL_WINDOWS = (2, 4, 8, 16)
POOL_GROUP = 256
POOL_BUF = 15
SSM_HEADS = 16
HEAD_DIM = 64
SSM_GROUPS = 2
HEADS_PER_GROUP = 8
GROUP_WIDTH = HEADS_PER_GROUP * HEAD_DIM
D_STATE = 128
CONV_WIDTH = 4
CONV_DIM = D_SSM + 2 * SSM_GROUPS * D_STATE
CHUNK = 128
D_FF = 4096
D_PLE = 256
PAST_LEN = 16384
EPS = 1e-6

LANES = 128
SUBLANES = 8
DT_PAD = LANES
D_IN_PAD = D_POOL + D_SSM + CONV_DIM + DT_PAD
POOL_HIST = 16
CONV_HIST = SUBLANES
VMEM_LIMIT = 56 * 1024 * 1024


def _rms(x, g):
    return x * lax.rsqrt(jnp.mean(x * x, axis=-1, keepdims=True) + EPS) * g


def _silu(x):
    return x * jax.nn.sigmoid(x)


def _softplus(x):
    return jnp.maximum(x, 0.0) + jnp.log1p(jnp.exp(-jnp.abs(x)))


def _neg_exp_heads(a_log_row):
    lane = lax.broadcasted_iota(jnp.int32, a_log_row.shape, a_log_row.ndim - 1)
    return jnp.where(lane < SSM_HEADS, -jnp.exp(a_log_row), 0.0)


def _dot(a, b):
    return jnp.dot(a.astype(BF16), b.astype(BF16), preferred_element_type=F32)


def _dot_exact(a, b):
    return jnp.dot(a, b, precision=HIGHEST, preferred_element_type=F32)


def _in_proj_kernel(x_ref, g_ref, w_ref, u_ref, z_ref, xbc_ref, dt_ref):
    xn = _rms(x_ref[...], g_ref[...]).astype(BF16)
    u_ref[...] = jnp.dot(xn, w_ref[:, 0:D_POOL], preferred_element_type=F32)
    z_ref[...] = jnp.dot(xn, w_ref[:, D_POOL:D_POOL + D_SSM], preferred_element_type=F32)
    xbc_ref[...] = jnp.dot(xn, w_ref[:, D_POOL + D_SSM:D_POOL + D_SSM + CONV_DIM],
                           preferred_element_type=F32)
    dt_ref[...] = jnp.dot(xn, w_ref[:, D_POOL + D_SSM + CONV_DIM:D_IN_PAD],
                          preferred_element_type=F32)


def _in_proj(x2d, g, w_in_pad, tm):
    n = x2d.shape[0]
    row = lambda i: (i, 0)
    fixed = lambda i: (0, 0)
    return pl.pallas_call(
        _in_proj_kernel,
        grid=(n // tm,),
        in_specs=[
            pl.BlockSpec((tm, D_MODEL), row),
            pl.BlockSpec((1, D_MODEL), fixed),
            pl.BlockSpec((D_MODEL, D_IN_PAD), fixed),
        ],
        out_specs=[
            pl.BlockSpec((tm, D_POOL), row),
            pl.BlockSpec((tm, D_SSM), row),
            pl.BlockSpec((tm, CONV_DIM), row),
            pl.BlockSpec((tm, DT_PAD), row),
        ],
        out_shape=[
            jax.ShapeDtypeStruct((n, D_POOL), F32),
            jax.ShapeDtypeStruct((n, D_SSM), F32),
            jax.ShapeDtypeStruct((n, CONV_DIM), F32),
            jax.ShapeDtypeStruct((n, DT_PAD), F32),
        ],
        compiler_params=pltpu.CompilerParams(
            dimension_semantics=("arbitrary",), vmem_limit_bytes=VMEM_LIMIT),
        name="in_proj",
    )(x2d, g, w_in_pad)


def _out_ffn_kernel(x_ref, mix_ref, p_ref, wout_ref, g_mlp_ref, w1_ref, w2_ref, g_ple_ref,
                    wg_ref, wple_ref, g_fin_ref, y_ref):
    h = x_ref[...] + jnp.dot(mix_ref[...].astype(BF16), wout_ref[...], preferred_element_type=F32)
    hn = _rms(h, g_mlp_ref[...]).astype(BF16)
    ff_block = D_FF // 4
    for c in range(4):
        f = jnp.dot(hn, w1_ref[:, c * ff_block:(c + 1) * ff_block], preferred_element_type=F32)
        f = jnp.square(jnp.maximum(f, 0.0)).astype(BF16)
        h = h + jnp.dot(f, w2_ref[c * ff_block:(c + 1) * ff_block, :], preferred_element_type=F32)
    gate = jax.nn.sigmoid(
        jnp.dot(_rms(h, g_ple_ref[...]).astype(BF16), wg_ref[...], preferred_element_type=F32))
    h = h + gate * jnp.dot(p_ref[...].astype(BF16), wple_ref[...], preferred_element_type=F32)
    y_ref[...] = _rms(h, g_fin_ref[...])


def _out_ffn(x2d, mix2d, p2d, wout, g_mlp, w1, w2, g_ple, wg, wple, g_fin, tm):
    n = x2d.shape[0]
    row = lambda i: (i, 0)
    fixed = lambda i: (0, 0)

    def resident(shape):
        return pl.BlockSpec(shape, fixed, pipeline_mode=pl.Buffered(1))

    return pl.pallas_call(
        _out_ffn_kernel,
        grid=(n // tm,),
        in_specs=[
            pl.BlockSpec((tm, D_MODEL), row),
            pl.BlockSpec((tm, 2 * D_MODEL), row),
            pl.BlockSpec((tm, D_PLE), row),
            resident((2 * D_MODEL, D_MODEL)),
            resident((1, D_MODEL)),
            resident((D_MODEL, D_FF)),
            resident((D_FF, D_MODEL)),
            resident((1, D_MODEL)),
            resident((D_MODEL, D_MODEL)),
            resident((D_PLE, D_MODEL)),
            resident((1, D_MODEL)),
        ],
        out_specs=pl.BlockSpec((tm, D_MODEL), row),
        out_shape=jax.ShapeDtypeStruct((n, D_MODEL), F32),
        compiler_params=pltpu.CompilerParams(
            dimension_semantics=("arbitrary",), vmem_limit_bytes=VMEM_LIMIT),
        name="out_ffn",
    )(x2d, mix2d, p2d, wout, g_mlp, w1, w2, g_ple, wg, wple, g_fin)


def _prompt_mix_kernel(u_ref, z_ref, xbc_ref, dt_ref, wpool_ref, pscale_ref, convw_ref, convb_ref,
                       dtb_ref, alog_ref, dskip_ref, gssm_ref, tri_ref, expand_ref,
                       mix_ref, ssm_ref, pool_hist, conv_hist, state_t):
    c = pl.program_id(1)
    n_chunks = pl.num_programs(1)

    @pl.when(c == 0)
    def _():
        pool_hist[...] = jnp.zeros_like(pool_hist)
        conv_hist[...] = jnp.zeros_like(conv_hist)
        state_t[...] = jnp.zeros_like(state_t)

    u = u_ref[...]
    ext = jnp.concatenate([pool_hist[...], u], axis=0)
    pool_hist[...] = u[CHUNK - POOL_HIST:, :]
    pos = c * CHUNK + lax.broadcasted_iota(jnp.int32, (CHUNK, 1), 0)
    y_pool = []
    for g, w in enumerate(POOL_WINDOWS):
        cols = slice(g * POOL_GROUP, (g + 1) * POOL_GROUP)
        s = ext[:, cols]
        shift = 1
        while shift < w:
            s = s + pltpu.roll(s, shift, axis=0)
            shift *= 2
        count = jnp.minimum(w, pos + 1).astype(F32)
        d = s[POOL_HIST:, :] / count - u[:, cols]
        y_pool.append(_dot(d, wpool_ref[g]))
    mix_ref[:, 0:D_POOL] = (jnp.concatenate(y_pool, axis=1) * pscale_ref[...]).astype(mix_ref.dtype)

    xbc = xbc_ref[...]
    extc = jnp.concatenate([conv_hist[...], xbc], axis=0)
    conv_hist[...] = xbc[CHUNK - CONV_HIST:, :]
    acc = convb_ref[...]
    for k in range(CONV_WIDTH):
        back = CONV_WIDTH - 1 - k
        tap = extc if back == 0 else pltpu.roll(extc, back, axis=0)
        acc = acc + tap[CONV_HIST:, :] * convw_ref[k:k + 1, :]
    xc = _silu(acc)
    xs = xc[:, 0:D_SSM]
    b_all = xc[:, D_SSM:D_SSM + SSM_GROUPS * D_STATE]
    c_all = xc[:, D_SSM + SSM_GROUPS * D_STATE:CONV_DIM]

    dt = _softplus(dt_ref[...] + dtb_ref[...])
    a_cum = _dot_exact(tri_ref[...], dt * _neg_exp_heads(alog_ref[...]))
    a_cum_t = a_cum.T
    a_last = a_cum[CHUNK - 1:CHUNK, :]
    expand = expand_ref[...]
    dt_x = _dot_exact(dt, expand)
    ea_x = _dot_exact(jnp.exp(a_cum), expand)
    de_x = _dot_exact(jnp.exp(a_last - a_cum), expand)
    cd_x = _dot_exact(jnp.broadcast_to(jnp.exp(a_last), (SUBLANES, DT_PAD)), expand)[0:1, :]
    x_dt = xs * dt_x
    x_w = x_dt * de_x

    row_i = lax.broadcasted_iota(jnp.int32, (CHUNK, CHUNK), 0)
    col_i = lax.broadcasted_iota(jnp.int32, (CHUNK, CHUNK), 1)
    causal = row_i >= col_i
    lane = lax.broadcasted_iota(jnp.int32, (CHUNK, 2 * HEAD_DIM), 1)
    first_head = lane < HEAD_DIM

    y_parts = []
    for g in range(SSM_GROUPS):
        gch = slice(g * GROUP_WIDTH, (g + 1) * GROUP_WIDTH)
        b_g = b_all[:, g * D_STATE:(g + 1) * D_STATE].astype(BF16)
        c_g = c_all[:, g * D_STATE:(g + 1) * D_STATE].astype(BF16)
        cb = lax.dot_general(c_g, b_g, (((1,), (1,)), ((), ())), preferred_element_type=F32)
        st_g = state_t[:, gch]
        y_off = jnp.dot(c_g, st_g.astype(BF16), preferred_element_type=F32) * ea_x[:, gch]
        y_diag = []
        for j in range(HEADS_PER_GROUP // 2):
            h0 = g * HEADS_PER_GROUP + 2 * j
            m = []
            for h in (h0, h0 + 1):
                seg = a_cum[:, h:h + 1] - a_cum_t[h:h + 1, :]
                m.append((cb * jnp.exp(jnp.where(causal, seg, -jnp.inf))).astype(BF16))
            pair = x_dt[:, h0 * HEAD_DIM:(h0 + 2) * HEAD_DIM]
            rhs = jnp.concatenate([jnp.where(first_head, pair, 0.0),
                                   jnp.where(first_head, 0.0, pair)], axis=0).astype(BF16)
            y_diag.append(jnp.dot(jnp.concatenate(m, axis=1), rhs, preferred_element_type=F32))
        y_parts.append(jnp.concatenate(y_diag, axis=1) + y_off)
        upd = lax.dot_general(b_g, x_w[:, gch].astype(BF16), (((0,), (0,)), ((), ())),
                              preferred_element_type=F32)
        state_t[:, gch] = st_g * cd_x[:, gch] + upd

    y = jnp.concatenate(y_parts, axis=1) + dskip_ref[...] * xs
    y = y * _silu(z_ref[...])
    mix_ref[:, D_POOL:] = _rms(y, gssm_ref[...]).astype(mix_ref.dtype)

    @pl.when(c == n_chunks - 1)
    def _():
        for j in range(D_SSM // LANES):
            ssm_ref[j * LANES:(j + 1) * LANES, :] = state_t[:, j * LANES:(j + 1) * LANES].T


def _prompt_mix(u, z, xbc, dt, consts, batch, seq):
    n_chunks = seq // CHUNK
    row = lambda b, c: (b * n_chunks + c, 0)
    fixed2 = lambda b, c: (0, 0)
    fixed3 = lambda b, c: (0, 0, 0)
    (wpool, pscale, convw, convb, dtb, a_row, dskip_x, gssm, tri, expand) = consts
    return pl.pallas_call(
        _prompt_mix_kernel,
        grid=(batch, n_chunks),
        in_specs=[
            pl.BlockSpec((CHUNK, D_POOL), row),
            pl.BlockSpec((CHUNK, D_SSM), row),
            pl.BlockSpec((CHUNK, CONV_DIM), row),
            pl.BlockSpec((CHUNK, DT_PAD), row),
            pl.BlockSpec((len(POOL_WINDOWS), POOL_GROUP, POOL_GROUP), fixed3),
            pl.BlockSpec((1, D_POOL), fixed2),
            pl.BlockSpec((CONV_WIDTH, CONV_DIM), fixed2),
            pl.BlockSpec((1, CONV_DIM), fixed2),
            pl.BlockSpec((1, DT_PAD), fixed2),
            pl.BlockSpec((1, DT_PAD), fixed2),
            pl.BlockSpec((1, D_SSM), fixed2),
            pl.BlockSpec((1, D_SSM), fixed2),
            pl.BlockSpec((CHUNK, CHUNK), fixed2),
            pl.BlockSpec((DT_PAD, D_SSM), fixed2),
        ],
        out_specs=[
            pl.BlockSpec((CHUNK, 2 * D_MODEL), row),
            pl.BlockSpec((None, D_SSM, D_STATE), lambda b, c: (b, 0, 0)),
        ],
        out_shape=[
            jax.ShapeDtypeStruct((batch * seq, 2 * D_MODEL), BF16),
            jax.ShapeDtypeStruct((batch, D_SSM, D_STATE), F32),
        ],
        scratch_shapes=[
            pltpu.VMEM((POOL_HIST, D_POOL), F32),
            pltpu.VMEM((CONV_HIST, CONV_DIM), F32),
            pltpu.VMEM((D_STATE, D_SSM), F32),
        ],
        compiler_params=pltpu.CompilerParams(
            dimension_semantics=("arbitrary", "arbitrary"), vmem_limit_bytes=VMEM_LIMIT),
        name="prompt_mix",
    )(u, z, xbc, dt, wpool, pscale, convw, convb, dtb, a_row, dskip_x, gssm, tri, expand)


SAMPLE_T = 4
SAMPLE_BB = 8


def _split3(x):
    hi = x.astype(BF16)
    r1 = x - hi.astype(F32)
    mid = r1.astype(BF16)
    lo = (r1 - mid.astype(F32)).astype(BF16)
    return hi, mid, lo


def _sample_mix_kernel(u_ref, z_ref, xbc_ref, dt_ref, pbuf_ref, cbuf_ref, state_ref,
                       wpool_ref, pscale_ref, convw_ref, convb_ref, dtb_ref, alog_ref, dskip_ref,
                       gssm_ref, expand_ref,
                       mix_ref, npool_ref, nconv_ref, nstate_ref):
    nt, bb = SAMPLE_T, SAMPLE_BB
    rows = nt * bb

    ext = [pbuf_ref[i] for i in range(POOL_BUF)] + [u_ref[t] for t in range(nt)]
    for i in range(POOL_BUF):
        npool_ref[i] = ext[i + nt]
    y_pool = []
    for g, w in enumerate(POOL_WINDOWS):
        cols = slice(g * POOL_GROUP, (g + 1) * POOL_GROUP)
        d = []
        for t in range(nt):
            hi = POOL_BUF + t
            win = ext[hi - w + 1][:, cols]
            for i in range(hi - w + 2, hi + 1):
                win = win + ext[i][:, cols]
            count = float(min(w, PAST_LEN + t + 1))
            d.append(win / count - ext[hi][:, cols])
        y_pool.append(_dot(jnp.concatenate(d, axis=0), wpool_ref[g]))
    y_pool = jnp.concatenate(y_pool, axis=1) * pscale_ref[...]

    extc = [cbuf_ref[i] for i in range(CONV_WIDTH - 1)] + [xbc_ref[t] for t in range(nt)]
    for j in range(CONV_WIDTH - 1):
        nconv_ref[j] = extc[nt + j]
    xc = []
    for t in range(nt):
        acc = convb_ref[...]
        for k in range(CONV_WIDTH):
            acc = acc + extc[t + k] * convw_ref[k:k + 1, :]
        xc.append(_silu(acc))
    xs = [v[:, 0:D_SSM] for v in xc]
    bm = [v[:, D_SSM:D_SSM + SSM_GROUPS * D_STATE] for v in xc]
    cm = [v[:, D_SSM + SSM_GROUPS * D_STATE:CONV_DIM] for v in xc]

    a_row = _neg_exp_heads(alog_ref[...])
    dt = [_softplus(dt_ref[t] + dtb_ref[...]) for t in range(nt)]
    a_cum = []
    for t in range(nt):
        a_t = dt[t] * a_row
        a_cum.append(a_t if t == 0 else a_cum[-1] + a_t)
    lane = lax.broadcasted_iota(jnp.int32, (bb, DT_PAD), 1)
    in_group0 = lane < HEADS_PER_GROUP
    head_rows = []
    pair_index = {}
    for l in range(nt):
        for s in range(l + 1):
            cb = [jnp.sum(cm[l][:, g * D_STATE:(g + 1) * D_STATE] * bm[s][:, g * D_STATE:(g + 1) * D_STATE],
                          axis=-1, keepdims=True) for g in range(SSM_GROUPS)]
            cb_heads = jnp.where(in_group0, cb[0], cb[1])
            pair_index[(l, s)] = len(head_rows)
            head_rows.append(cb_heads * jnp.exp(a_cum[l] - a_cum[s]) * dt[s])
    ea_index = len(head_rows)
    head_rows += [jnp.exp(a_cum[t]) for t in range(nt)]
    w_index = len(head_rows)
    head_rows += [jnp.exp(a_cum[nt - 1] - a_cum[t]) * dt[t] for t in range(nt)]
    cd_index = len(head_rows)
    head_rows.append(jnp.exp(a_cum[nt - 1]))
    chan = _dot_exact(jnp.concatenate(head_rows, axis=0), expand_ref[...])
    chan_rows = lambda i: chan[i * bb:(i + 1) * bb, :]

    y_diag = []
    for l in range(nt):
        acc = chan_rows(pair_index[(l, 0)]) * xs[0]
        for s in range(1, l + 1):
            acc = acc + chan_rows(pair_index[(l, s)]) * xs[s]
        y_diag.append(acc)
    xs_all = jnp.concatenate(xs, axis=0)
    xw_all = jnp.concatenate([chan_rows(w_index + t) * xs[t] for t in range(nt)], axis=0)
    ea_all = jnp.concatenate([chan_rows(ea_index + t) for t in range(nt)], axis=0)
    b_all = jnp.concatenate(bm, axis=0).astype(BF16)
    c_all = jnp.concatenate(cm, axis=0).astype(BF16)
    cd3 = jnp.concatenate([p.astype(F32) for p in _split3(chan_rows(cd_index))], axis=0)
    ones3 = jnp.ones((3 * bb, D_STATE), BF16)
    row_batch = lax.broadcasted_iota(jnp.int32, (rows, 1), 0) & (bb - 1)
    row_batch3 = lax.broadcasted_iota(jnp.int32, (3 * bb, 1), 0) & (bb - 1)
    contract_rows = (((0,), (0,)), ((), ()))

    def per_batch(b, y_off):
        st = state_ref[b]
        st16 = st.astype(BF16)
        mine = row_batch == b
        xw_b = jnp.where(mine, xw_all, 0.0).astype(BF16)
        cd_b = jnp.where(row_batch3 == b, cd3, 0.0).astype(BF16)
        scale = lax.dot_general(cd_b, ones3, contract_rows, preferred_element_type=F32)
        yo = []
        for g in range(SSM_GROUPS):
            grow = slice(g * GROUP_WIDTH, (g + 1) * GROUP_WIDTH)
            gst = slice(g * D_STATE, (g + 1) * D_STATE)
            yo.append(lax.dot_general(c_all[:, gst], st16[grow, :], (((1,), (1,)), ((), ())),
                                      preferred_element_type=F32))
            upd = lax.dot_general(xw_b[:, grow], b_all[:, gst], contract_rows,
                                  preferred_element_type=F32)
            nstate_ref[b, grow, :] = st[grow, :] * scale[grow, :] + upd
        return jnp.where(mine, jnp.concatenate(yo, axis=1), y_off)

    y_off = lax.fori_loop(0, bb, per_batch, jnp.zeros((rows, D_SSM), F32))
    y = jnp.concatenate(y_diag, axis=0) + y_off * ea_all + dskip_ref[...] * xs_all
    y = y * _silu(jnp.concatenate([z_ref[t] for t in range(nt)], axis=0))
    y_ssm = _rms(y, gssm_ref[...])
    for t in range(nt):
        mix_ref[t, :, 0:D_POOL] = y_pool[t * bb:(t + 1) * bb, :].astype(mix_ref.dtype)
        mix_ref[t, :, D_POOL:] = y_ssm[t * bb:(t + 1) * bb, :].astype(mix_ref.dtype)


def _sample_group_mix(u, z, xbc, dt, pbuf, cbuf, state, consts):
    nt, bb = SAMPLE_T, SAMPLE_BB
    batch = u.shape[1]
    (wpool, pscale, convw, convb, dtb, alog, dskip_x, gssm, _, expand) = consts
    blk = lambda i: (0, i, 0)
    fixed2 = lambda i: (0, 0)
    fixed3 = lambda i: (0, 0, 0)
    return pl.pallas_call(
        _sample_mix_kernel,
        grid=(batch // bb,),
        in_specs=[
            pl.BlockSpec((nt, bb, D_POOL), blk),
            pl.BlockSpec((nt, bb, D_SSM), blk),
            pl.BlockSpec((nt, bb, CONV_DIM), blk),
            pl.BlockSpec((nt, bb, DT_PAD), blk),
            pl.BlockSpec((POOL_BUF, bb, D_POOL), blk),
            pl.BlockSpec((CONV_WIDTH - 1, bb, CONV_DIM), blk),
            pl.BlockSpec((bb, D_SSM, D_STATE), lambda i: (i, 0, 0)),
            pl.BlockSpec((len(POOL_WINDOWS), POOL_GROUP, POOL_GROUP), fixed3),
            pl.BlockSpec((1, D_POOL), fixed2),
            pl.BlockSpec((CONV_WIDTH, CONV_DIM), fixed2),
            pl.BlockSpec((1, CONV_DIM), fixed2),
            pl.BlockSpec((1, DT_PAD), fixed2),
            pl.BlockSpec((1, DT_PAD), fixed2),
            pl.BlockSpec((1, D_SSM), fixed2),
            pl.BlockSpec((1, D_SSM), fixed2),
            pl.BlockSpec((DT_PAD, D_SSM), fixed2),
        ],
        out_specs=[
            pl.BlockSpec((nt, bb, 2 * D_MODEL), blk),
            pl.BlockSpec((POOL_BUF, bb, D_POOL), blk),
            pl.BlockSpec((CONV_WIDTH - 1, bb, CONV_DIM), blk),
            pl.BlockSpec((bb, D_SSM, D_STATE), lambda i: (i, 0, 0)),
        ],
        out_shape=[
            jax.ShapeDtypeStruct((nt, batch, 2 * D_MODEL), F32),
            jax.ShapeDtypeStruct((POOL_BUF, batch, D_POOL), F32),
            jax.ShapeDtypeStruct((CONV_WIDTH - 1, batch, CONV_DIM), F32),
            jax.ShapeDtypeStruct((batch, D_SSM, D_STATE), F32),
        ],
        compiler_params=pltpu.CompilerParams(
            dimension_semantics=("arbitrary",), vmem_limit_bytes=VMEM_LIMIT),
        name="sample_mix",
    )(u, z, xbc, dt, pbuf, cbuf, state, wpool, pscale, convw, convb, dtb, alog, dskip_x, gssm, expand)


def _mix_constants(w_pool, pool_scale, conv_w, conv_b, dt_bias, a_log, d_skip, ssm_norm_g):
    pad = DT_PAD - SSM_HEADS
    head_of_channel = jnp.arange(D_SSM, dtype=jnp.int32) // HEAD_DIM
    expand = (jnp.arange(DT_PAD, dtype=jnp.int32)[:, None] == head_of_channel[None, :]).astype(F32)
    tri = (jnp.arange(CHUNK)[:, None] >= jnp.arange(CHUNK)[None, :]).astype(F32)
    return (
        w_pool.astype(BF16),
        pool_scale.reshape(1, D_POOL),
        conv_w,
        conv_b.reshape(1, CONV_DIM),
        jnp.pad(dt_bias, (0, pad)).reshape(1, DT_PAD),
        jnp.pad(a_log, (0, pad)).reshape(1, DT_PAD),
        jnp.repeat(d_skip, HEAD_DIM).reshape(1, D_SSM),
        ssm_norm_g.reshape(1, D_SSM),
        tri,
        expand,
    )


PROMPT_IN_TM = 512
PROMPT_FFN_TM = 256
SAMPLE_TM = 256


def kernel(x_prompt, x_sample, p_prompt, p_sample, state_pool, state_conv, state_ssm, w_in, w_pool, pool_scale, conv_w, conv_b, dt_bias, a_log, d_skip, ssm_norm_g, w_out, norm_mix_g, norm_mlp_g, w_ff1, w_ff2, norm_ple_g, w_gate, w_ple, final_norm_g):
    assert w_in.shape[0] == 1, "single-layer trunk"
    bp, tp = x_prompt.shape[:2]
    bs, ts = x_sample.shape[:2]
    assert tp % CHUNK == 0 and tp >= POOL_BUF and ts == SAMPLE_T and bs % SAMPLE_BB == 0

    row = lambda v: v.reshape(1, -1)
    w_in_pad = jnp.pad(w_in[0], ((0, 0), (0, D_IN_PAD - w_in.shape[2]))).astype(BF16)
    consts = _mix_constants(w_pool[0], pool_scale[0], conv_w[0], conv_b[0], dt_bias[0], a_log[0],
                            d_skip[0], ssm_norm_g[0])
    ffn_weights = (w_out[0].astype(BF16), row(norm_mlp_g[0]), w_ff1[0].astype(BF16),
                   w_ff2[0].astype(BF16), row(norm_ple_g[0]), w_gate[0].astype(BF16),
                   w_ple[0].astype(BF16), row(final_norm_g))
    g_mix = row(norm_mix_g[0])

    xp = x_prompt.reshape(bp * tp, D_MODEL)
    u, z, xbc, dt = _in_proj(xp, g_mix, w_in_pad, PROMPT_IN_TM)
    mix, ssm_p = _prompt_mix(u, z, xbc, dt, consts, bp, tp)
    y_prompt = _out_ffn(xp, mix, p_prompt[0].reshape(bp * tp, D_PLE), *ffn_weights, PROMPT_FFN_TM)
    pool_p = u.reshape(bp, tp, D_POOL)[:, tp - POOL_BUF:, :]
    conv_p = xbc.reshape(bp, tp, CONV_DIM)[:, tp - (CONV_WIDTH - 1):, :]

    tmaj = lambda a: jnp.transpose(a, (1, 0, 2))
    xs_t = tmaj(x_sample).reshape(ts * bs, D_MODEL)
    u_s, z_s, xbc_s, dt_s = _in_proj(xs_t, g_mix, w_in_pad, SAMPLE_TM)
    unflat = lambda a: a.reshape(ts, bs, a.shape[-1])
    mix_s, pool_s, conv_s, ssm_s = _sample_group_mix(
        unflat(u_s), unflat(z_s), unflat(xbc_s), unflat(dt_s), tmaj(state_pool[0]), tmaj(state_conv[0]),
        state_ssm[0].reshape(bs, D_SSM, D_STATE), consts)
    y_s = _out_ffn(xs_t, mix_s.reshape(ts * bs, 2 * D_MODEL), tmaj(p_sample[0]).reshape(ts * bs, D_PLE),
                   *ffn_weights, SAMPLE_TM)
    y_sample = tmaj(y_s.reshape(ts, bs, D_MODEL))

    state_shape = (1, -1, SSM_HEADS, HEAD_DIM, D_STATE)
    return (y_prompt.reshape(bp, tp, D_MODEL), y_sample,
            pool_p[None], conv_p[None], ssm_p.reshape(state_shape),
            tmaj(pool_s)[None], tmaj(conv_s)[None], ssm_s.reshape(state_shape))
```

```python
import functools
import math

import jax
import jax.numpy as jnp
from jax import lax
from jax.experimental import pallas as pl
from jax.experimental.pallas import tpu as pltpu

F32 = jnp.float32
BF16 = jnp.bfloat16

D_MODEL = 1024
D_POOL = 1024
D_SSM = 1024
POOL_WINDOWS = (2, 4, 8, 16)
POOL_GROUP = 256
POOL_BUF = 15
SSM_HEADS = 16
HEAD_DIM = 64
SSM_GROUPS = 2
HEADS_PER_GROUP = 8
GROUP_WIDTH = HEADS_PER_GROUP * HEAD_DIM
D_STATE = 128
CONV_WIDTH = 4
CONV_DIM = D_SSM + 2 * SSM_GROUPS * D_STATE
CHUNK = 128
D_FF = 4096
D_PLE = 256
PAST_LEN = 16384
EPS = 1e-6

LANES = 128
SUBLANES = 8
DT_PAD = LANES
HEAD_SLOT = SSM_HEADS
D_IN_PAD = D_POOL + D_SSM + CONV_DIM + DT_PAD
POOL_HIST = 16
CONV_HIST = SUBLANES
VMEM_LIMIT = 56 * 1024 * 1024


def _rms(x, g):
    return x * lax.rsqrt(jnp.mean(x * x, axis=-1, keepdims=True) + EPS) * g


def _silu(x):
    return x * jax.nn.sigmoid(x)


def _softplus(x):
    return jnp.maximum(x, 0.0) + jnp.log1p(jnp.exp(-jnp.abs(x)))


def _neg_exp_heads(a_log_row):
    lane = lax.broadcasted_iota(jnp.int32, a_log_row.shape, a_log_row.ndim - 1)
    return jnp.where(lane < SSM_HEADS, -jnp.exp(a_log_row), 0.0)


def _dot(a, b):
    return jnp.dot(a.astype(BF16), b.astype(BF16), preferred_element_type=F32)


def _split3(x):
    hi = x.astype(BF16)
    r1 = x - hi.astype(F32)
    mid = r1.astype(BF16)
    lo = (r1 - mid.astype(F32)).astype(BF16)
    return hi, mid, lo


def _pack_heads(v):
    lane = lax.broadcasted_iota(jnp.int32, v.shape, 1)
    hi, mid, lo = _split3(jnp.where(lane < SSM_HEADS, v, 0.0))
    packed = (hi.astype(F32) + pltpu.roll(mid.astype(F32), HEAD_SLOT, axis=1)
              + pltpu.roll(lo.astype(F32), 2 * HEAD_SLOT, axis=1))
    return packed.astype(BF16)


def _expand_heads(v, expand3):
    return jnp.dot(_pack_heads(v), expand3, preferred_element_type=F32)


def _chunk_cumsum(tri16, a):
    s = jnp.dot(tri16, _pack_heads(a), preferred_element_type=F32)
    return s + pltpu.roll(s, DT_PAD - HEAD_SLOT, axis=1) + pltpu.roll(s, DT_PAD - 2 * HEAD_SLOT, axis=1)


def _in_proj_kernel(x_ref, g_ref, w_ref, u_ref, z_ref, xbc_ref, dt_ref):
    xn = _rms(x_ref[...], g_ref[...]).astype(BF16)
    u_ref[...] = jnp.dot(xn, w_ref[:, 0:D_POOL], preferred_element_type=F32)
    z_ref[...] = jnp.dot(xn, w_ref[:, D_POOL:D_POOL + D_SSM], preferred_element_type=F32)
    xbc_ref[...] = jnp.dot(xn, w_ref[:, D_POOL + D_SSM:D_POOL + D_SSM + CONV_DIM],
                           preferred_element_type=F32)
    dt_ref[...] = jnp.dot(xn, w_ref[:, D_POOL + D_SSM + CONV_DIM:D_IN_PAD],
                          preferred_element_type=F32)


def _in_proj(x2d, g, w_in_pad, tm):
    n = x2d.shape[0]
    row = lambda i: (i, 0)
    fixed = lambda i: (0, 0)
    return pl.pallas_call(
        _in_proj_kernel,
        grid=(n // tm,),
        in_specs=[
            pl.BlockSpec((tm, D_MODEL), row),
            pl.BlockSpec((1, D_MODEL), fixed),
            pl.BlockSpec((D_MODEL, D_IN_PAD), fixed),
        ],
        out_specs=[
            pl.BlockSpec((tm, D_POOL), row),
            pl.BlockSpec((tm, D_SSM), row),
            pl.BlockSpec((tm, CONV_DIM), row),
            pl.BlockSpec((tm, DT_PAD), row),
        ],
        out_shape=[
            jax.ShapeDtypeStruct((n, D_POOL), F32),
            jax.ShapeDtypeStruct((n, D_SSM), F32),
            jax.ShapeDtypeStruct((n, CONV_DIM), F32),
            jax.ShapeDtypeStruct((n, DT_PAD), F32),
        ],
        compiler_params=pltpu.CompilerParams(
            dimension_semantics=("arbitrary",), vmem_limit_bytes=VMEM_LIMIT),
        name="in_proj",
    )(x2d, g, w_in_pad)


def _out_ffn_kernel(x_ref, mix_ref, p_ref, wout_ref, g_mlp_ref, w1_ref, w2_ref, g_ple_ref,
                    wg_ref, wple_ref, g_fin_ref, y_ref):
    h = x_ref[...] + jnp.dot(mix_ref[...].astype(BF16), wout_ref[...], preferred_element_type=F32)
    hn = _rms(h, g_mlp_ref[...]).astype(BF16)
    ff_block = D_FF // 4
    for c in range(4):
        f = jnp.dot(hn, w1_ref[:, c * ff_block:(c + 1) * ff_block], preferred_element_type=F32)
        f = jnp.square(jnp.maximum(f, 0.0)).astype(BF16)
        h = h + jnp.dot(f, w2_ref[c * ff_block:(c + 1) * ff_block, :], preferred_element_type=F32)
    gate = jax.nn.sigmoid(
        jnp.dot(_rms(h, g_ple_ref[...]).astype(BF16), wg_ref[...], preferred_element_type=F32))
    h = h + gate * jnp.dot(p_ref[...].astype(BF16), wple_ref[...], preferred_element_type=F32)
    y_ref[...] = _rms(h, g_fin_ref[...])


def _out_ffn(x2d, mix2d, p2d, wout, g_mlp, w1, w2, g_ple, wg, wple, g_fin, tm):
    n = x2d.shape[0]
    row = lambda i: (i, 0)
    fixed = lambda i: (0, 0)

    def resident(shape):
        return pl.BlockSpec(shape, fixed, pipeline_mode=pl.Buffered(1))

    return pl.pallas_call(
        _out_ffn_kernel,
        grid=(n // tm,),
        in_specs=[
            pl.BlockSpec((tm, D_MODEL), row),
            pl.BlockSpec((tm, 2 * D_MODEL), row),
            pl.BlockSpec((tm, D_PLE), row),
            resident((2 * D_MODEL, D_MODEL)),
            resident((1, D_MODEL)),
            resident((D_MODEL, D_FF)),
            resident((D_FF, D_MODEL)),
            resident((1, D_MODEL)),
            resident((D_MODEL, D_MODEL)),
            resident((D_PLE, D_MODEL)),
            resident((1, D_MODEL)),
        ],
        out_specs=pl.BlockSpec((tm, D_MODEL), row),
        out_shape=jax.ShapeDtypeStruct((n, D_MODEL), F32),
        compiler_params=pltpu.CompilerParams(
            dimension_semantics=("arbitrary",), vmem_limit_bytes=VMEM_LIMIT),
        name="out_ffn",
    )(x2d, mix2d, p2d, wout, g_mlp, w1, w2, g_ple, wg, wple, g_fin)


def _prompt_mix_kernel(u_ref, z_ref, xbc_ref, dt_ref, wpool_ref, pscale_ref, convw_ref, convb_ref,
                       dtb_ref, alog_ref, dskip_ref, gssm_ref, tri_ref, expand_ref,
                       mix_ref, ssm_ref, pool_hist, conv_hist, state_t):
    c = pl.program_id(1)
    n_chunks = pl.num_programs(1)

    @pl.when(c == 0)
    def _():
        pool_hist[...] = jnp.zeros_like(pool_hist)
        conv_hist[...] = jnp.zeros_like(conv_hist)
        state_t[...] = jnp.zeros_like(state_t)

    u = u_ref[...]
    ext = jnp.concatenate([pool_hist[...], u], axis=0)
    pool_hist[...] = u[CHUNK - POOL_HIST:, :]
    pos = c * CHUNK + lax.broadcasted_iota(jnp.int32, (CHUNK, 1), 0)
    y_pool = []
    for g, w in enumerate(POOL_WINDOWS):
        cols = slice(g * POOL_GROUP, (g + 1) * POOL_GROUP)
        s = ext[:, cols]
        shift = 1
        while shift < w:
            s = s + pltpu.roll(s, shift, axis=0)
            shift *= 2
        count = jnp.minimum(w, pos + 1).astype(F32)
        d = s[POOL_HIST:, :] / count - u[:, cols]
        y_pool.append(_dot(d, wpool_ref[g]))
    mix_ref[:, 0:D_POOL] = (jnp.concatenate(y_pool, axis=1) * pscale_ref[...]).astype(mix_ref.dtype)

    xbc = xbc_ref[...]
    extc = jnp.concatenate([conv_hist[...], xbc], axis=0)
    conv_hist[...] = xbc[CHUNK - CONV_HIST:, :]
    acc = convb_ref[...]
    for k in range(CONV_WIDTH):
        back = CONV_WIDTH - 1 - k
        tap = extc if back == 0 else pltpu.roll(extc, back, axis=0)
        acc = acc + tap[CONV_HIST:, :] * convw_ref[k:k + 1, :]
    xc = _silu(acc)
    xs = xc[:, 0:D_SSM]
    b_all = xc[:, D_SSM:D_SSM + SSM_GROUPS * D_STATE]
    c_all = xc[:, D_SSM + SSM_GROUPS * D_STATE:CONV_DIM]

    dt = _softplus(dt_ref[...] + dtb_ref[...])
    a_cum = _chunk_cumsum(tri_ref[...], dt * _neg_exp_heads(alog_ref[...]))
    a_cum_t = a_cum.T
    a_last = a_cum[CHUNK - 1:CHUNK, :]
    expand = expand_ref[...]
    dt_x = _expand_heads(dt, expand)
    ea_x = _expand_heads(jnp.exp(a_cum), expand)
    de_x = _expand_heads(jnp.exp(a_last - a_cum), expand)
    cd_x = _expand_heads(jnp.broadcast_to(jnp.exp(a_last), (2 * SUBLANES, DT_PAD)), expand)[0:1, :]
    x_dt = xs * dt_x
    x_w = x_dt * de_x

    row_i = lax.broadcasted_iota(jnp.int32, (CHUNK, CHUNK), 0)
    col_i = lax.broadcasted_iota(jnp.int32, (CHUNK, CHUNK), 1)
    causal = row_i >= col_i
    lane = lax.broadcasted_iota(jnp.int32, (CHUNK, 2 * HEAD_DIM), 1)
    first_head = lane < HEAD_DIM

    y_parts = []
    for g in range(SSM_GROUPS):
        gch = slice(g * GROUP_WIDTH, (g + 1) * GROUP_WIDTH)
        b_g = b_all[:, g * D_STATE:(g + 1) * D_STATE].astype(BF16)
        c_g = c_all[:, g * D_STATE:(g + 1) * D_STATE].astype(BF16)
        cb = lax.dot_general(c_g, b_g, (((1,), (1,)), ((), ())), preferred_element_type=F32)
        st_g = state_t[:, gch]
        y_off = jnp.dot(c_g, st_g.astype(BF16), preferred_element_type=F32) * ea_x[:, gch]
        y_diag = []
        for j in range(HEADS_PER_GROUP // 2):
            h0 = g * HEADS_PER_GROUP + 2 * j
            m = []
            for h in (h0, h0 + 1):
                seg = a_cum[:, h:h + 1] - a_cum_t[h:h + 1, :]
                m.append((cb * jnp.exp(jnp.where(causal, seg, -jnp.inf))).astype(BF16))
            pair = x_dt[:, h0 * HEAD_DIM:(h0 + 2) * HEAD_DIM]
            rhs = jnp.concatenate([jnp.where(first_head, pair, 0.0),
                                   jnp.where(first_head, 0.0, pair)], axis=0).astype(BF16)
            y_diag.append(jnp.dot(jnp.concatenate(m, axis=1), rhs, preferred_element_type=F32))
        y_parts.append(jnp.concatenate(y_diag, axis=1) + y_off)
        upd = lax.dot_general(b_g, x_w[:, gch].astype(BF16), (((0,), (0,)), ((), ())),
                              preferred_element_type=F32)
        state_t[:, gch] = st_g * cd_x[:, gch] + upd

    y = jnp.concatenate(y_parts, axis=1) + dskip_ref[...] * xs
    y = y * _silu(z_ref[...])
    mix_ref[:, D_POOL:] = _rms(y, gssm_ref[...]).astype(mix_ref.dtype)

    @pl.when(c == n_chunks - 1)
    def _():
        for j in range(D_SSM // LANES):
            ssm_ref[j * LANES:(j + 1) * LANES, :] = state_t[:, j * LANES:(j + 1) * LANES].T


def _prompt_mix(u, z, xbc, dt, consts, batch, seq):
    n_chunks = seq // CHUNK
    row = lambda b, c: (b * n_chunks + c, 0)
    fixed2 = lambda b, c: (0, 0)
    fixed3 = lambda b, c: (0, 0, 0)
    (wpool, pscale, convw, convb, dtb, a_row, dskip_x, gssm, tri, expand) = consts
    return pl.pallas_call(
        _prompt_mix_kernel,
        grid=(batch, n_chunks),
        in_specs=[
            pl.BlockSpec((CHUNK, D_POOL), row),
            pl.BlockSpec((CHUNK, D_SSM), row),
            pl.BlockSpec((CHUNK, CONV_DIM), row),
            pl.BlockSpec((CHUNK, DT_PAD), row),
            pl.BlockSpec((len(POOL_WINDOWS), POOL_GROUP, POOL_GROUP), fixed3),
            pl.BlockSpec((1, D_POOL), fixed2),
            pl.BlockSpec((CONV_WIDTH, CONV_DIM), fixed2),
            pl.BlockSpec((1, CONV_DIM), fixed2),
            pl.BlockSpec((1, DT_PAD), fixed2),
            pl.BlockSpec((1, DT_PAD), fixed2),
            pl.BlockSpec((1, D_SSM), fixed2),
            pl.BlockSpec((1, D_SSM), fixed2),
            pl.BlockSpec((CHUNK, CHUNK), fixed2),
            pl.BlockSpec((DT_PAD, D_SSM), fixed2),
        ],
        out_specs=[
            pl.BlockSpec((CHUNK, 2 * D_MODEL), row),
            pl.BlockSpec((None, D_SSM, D_STATE), lambda b, c: (b, 0, 0)),
        ],
        out_shape=[
            jax.ShapeDtypeStruct((batch * seq, 2 * D_MODEL), BF16),
            jax.ShapeDtypeStruct((batch, D_SSM, D_STATE), F32),
        ],
        scratch_shapes=[
            pltpu.VMEM((POOL_HIST, D_POOL), F32),
            pltpu.VMEM((CONV_HIST, CONV_DIM), F32),
            pltpu.VMEM((D_STATE, D_SSM), F32),
        ],
        compiler_params=pltpu.CompilerParams(
            dimension_semantics=("arbitrary", "arbitrary"), vmem_limit_bytes=VMEM_LIMIT),
        name="prompt_mix",
    )(u, z, xbc, dt, wpool, pscale, convw, convb, dtb, a_row, dskip_x, gssm, tri, expand)


SAMPLE_T = 4
SAMPLE_BB = 8


def _sample_mix_kernel(u_ref, z_ref, xbc_ref, dt_ref, pbuf_ref, cbuf_ref, state_ref,
                       wpool_ref, pscale_ref, convw_ref, convb_ref, dtb_ref, alog_ref, dskip_ref,
                       gssm_ref, expand_ref,
                       mix_ref, npool_ref, nconv_ref, nstate_ref):
    nt, bb = SAMPLE_T, SAMPLE_BB
    rows = nt * bb

    ext = [pbuf_ref[i] for i in range(POOL_BUF)] + [u_ref[t] for t in range(nt)]
    for i in range(POOL_BUF):
        npool_ref[i] = ext[i + nt]
    y_pool = []
    for g, w in enumerate(POOL_WINDOWS):
        cols = slice(g * POOL_GROUP, (g + 1) * POOL_GROUP)
        d = []
        for t in range(nt):
            hi = POOL_BUF + t
            win = ext[hi - w + 1][:, cols]
            for i in range(hi - w + 2, hi + 1):
                win = win + ext[i][:, cols]
            count = float(min(w, PAST_LEN + t + 1))
            d.append(win / count - ext[hi][:, cols])
        y_pool.append(_dot(jnp.concatenate(d, axis=0), wpool_ref[g]))
    y_pool = jnp.concatenate(y_pool, axis=1) * pscale_ref[...]

    extc = [cbuf_ref[i] for i in range(CONV_WIDTH - 1)] + [xbc_ref[t] for t in range(nt)]
    for j in range(CONV_WIDTH - 1):
        nconv_ref[j] = extc[nt + j]
    xc = []
    for t in range(nt):
        acc = convb_ref[...]
        for k in range(CONV_WIDTH):
            acc = acc + extc[t + k] * convw_ref[k:k + 1, :]
        xc.append(_silu(acc))
    xs = [v[:, 0:D_SSM] for v in xc]
    bm = [v[:, D_SSM:D_SSM + SSM_GROUPS * D_STATE] for v in xc]
    cm = [v[:, D_SSM + SSM_GROUPS * D_STATE:CONV_DIM] for v in xc]

    a_row = _neg_exp_heads(alog_ref[...])
    dt = [_softplus(dt_ref[t] + dtb_ref[...]) for t in range(nt)]
    a_cum = []
    for t in range(nt):
        a_t = dt[t] * a_row
        a_cum.append(a_t if t == 0 else a_cum[-1] + a_t)
    lane = lax.broadcasted_iota(jnp.int32, (bb, DT_PAD), 1)
    in_group0 = lane < HEADS_PER_GROUP
    head_rows = []
    pair_index = {}
    for l in range(nt):
        for s in range(l + 1):
            cb = [jnp.sum(cm[l][:, g * D_STATE:(g + 1) * D_STATE] * bm[s][:, g * D_STATE:(g + 1) * D_STATE],
                          axis=-1, keepdims=True) for g in range(SSM_GROUPS)]
            cb_heads = jnp.where(in_group0, cb[0], cb[1])
            pair_index[(l, s)] = len(head_rows)
            head_rows.append(cb_heads * jnp.exp(a_cum[l] - a_cum[s]) * dt[s])
    ea_index = len(head_rows)
    head_rows += [jnp.exp(a_cum[t]) for t in range(nt)]
    w_index = len(head_rows)
    head_rows += [jnp.exp(a_cum[nt - 1] - a_cum[t]) * dt[t] for t in range(nt)]
    cd_index = len(head_rows)
    head_rows.append(jnp.exp(a_cum[nt - 1]))
    chan = _expand_heads(jnp.concatenate(head_rows, axis=0), expand_ref[...])
    chan_rows = lambda i: chan[i * bb:(i + 1) * bb, :]

    y_diag = []
    for l in range(nt):
        acc = chan_rows(pair_index[(l, 0)]) * xs[0]
        for s in range(1, l + 1):
            acc = acc + chan_rows(pair_index[(l, s)]) * xs[s]
        y_diag.append(acc)
    xs_all = jnp.concatenate(xs, axis=0)
    xw_all = jnp.concatenate([chan_rows(w_index + t) * xs[t] for t in range(nt)], axis=0)
    ea_all = jnp.concatenate([chan_rows(ea_index + t) for t in range(nt)], axis=0)
    b_all = jnp.concatenate(bm, axis=0).astype(BF16)
    c_all = jnp.concatenate(cm, axis=0).astype(BF16)
    cd3 = jnp.concatenate([p.astype(F32) for p in _split3(chan_rows(cd_index))], axis=0)
    ones3 = jnp.ones((3 * bb, D_STATE), BF16)
    row_batch = lax.broadcasted_iota(jnp.int32, (rows, 1), 0) & (bb - 1)
    row_batch3 = lax.broadcasted_iota(jnp.int32, (3 * bb, 1), 0) & (bb - 1)
    contract_rows = (((0,), (0,)), ((), ()))

    def per_batch(b, y_off):
        st = state_ref[b]
        st16 = st.astype(BF16)
        mine = row_batch == b
        xw_b = jnp.where(mine, xw_all, 0.0).astype(BF16)
        cd_b = jnp.where(row_batch3 == b, cd3, 0.0).astype(BF16)
        scale = lax.dot_general(cd_b, ones3, contract_rows, preferred_element_type=F32)
        yo = []
        for g in range(SSM_GROUPS):
            grow = slice(g * GROUP_WIDTH, (g + 1) * GROUP_WIDTH)
            gst = slice(g * D_STATE, (g + 1) * D_STATE)
            yo.append(lax.dot_general(c_all[:, gst], st16[grow, :], (((1,), (1,)), ((), ())),
                                      preferred_element_type=F32))
            upd = lax.dot_general(xw_b[:, grow], b_all[:, gst], contract_rows,
                                  preferred_element_type=F32)
            nstate_ref[b, grow, :] = st[grow, :] * scale[grow, :] + upd
        return jnp.where(mine, jnp.concatenate(yo, axis=1), y_off)

    y_off = lax.fori_loop(0, bb, per_batch, jnp.zeros((rows, D_SSM), F32))
    y = jnp.concatenate(y_diag, axis=0) + y_off * ea_all + dskip_ref[...] * xs_all
    y = y * _silu(jnp.concatenate([z_ref[t] for t in range(nt)], axis=0))
    y_ssm = _rms(y, gssm_ref[...])
    for t in range(nt):
        mix_ref[t, :, 0:D_POOL] = y_pool[t * bb:(t + 1) * bb, :].astype(mix_ref.dtype)
        mix_ref[t, :, D_POOL:] = y_ssm[t * bb:(t + 1) * bb, :].astype(mix_ref.dtype)


def _sample_group_mix(u, z, xbc, dt, pbuf, cbuf, state, consts):
    nt, bb = SAMPLE_T, SAMPLE_BB
    batch = u.shape[1]
    (wpool, pscale, convw, convb, dtb, alog, dskip_x, gssm, _, expand) = consts
    blk = lambda i: (0, i, 0)
    fixed2 = lambda i: (0, 0)
    fixed3 = lambda i: (0, 0, 0)
    return pl.pallas_call(
        _sample_mix_kernel,
        grid=(batch // bb,),
        in_specs=[
            pl.BlockSpec((nt, bb, D_POOL), blk),
            pl.BlockSpec((nt, bb, D_SSM), blk),
            pl.BlockSpec((nt, bb, CONV_DIM), blk),
            pl.BlockSpec((nt, bb, DT_PAD), blk),
            pl.BlockSpec((POOL_BUF, bb, D_POOL), blk),
            pl.BlockSpec((CONV_WIDTH - 1, bb, CONV_DIM), blk),
            pl.BlockSpec((bb, D_SSM, D_STATE), lambda i: (i, 0, 0)),
            pl.BlockSpec((len(POOL_WINDOWS), POOL_GROUP, POOL_GROUP), fixed3),
            pl.BlockSpec((1, D_POOL), fixed2),
            pl.BlockSpec((CONV_WIDTH, CONV_DIM), fixed2),
            pl.BlockSpec((1, CONV_DIM), fixed2),
            pl.BlockSpec((1, DT_PAD), fixed2),
            pl.BlockSpec((1, DT_PAD), fixed2),
            pl.BlockSpec((1, D_SSM), fixed2),
            pl.BlockSpec((1, D_SSM), fixed2),
            pl.BlockSpec((DT_PAD, D_SSM), fixed2),
        ],
        out_specs=[
            pl.BlockSpec((nt, bb, 2 * D_MODEL), blk),
            pl.BlockSpec((POOL_BUF, bb, D_POOL), blk),
            pl.BlockSpec((CONV_WIDTH - 1, bb, CONV_DIM), blk),
            pl.BlockSpec((bb, D_SSM, D_STATE), lambda i: (i, 0, 0)),
        ],
        out_shape=[
            jax.ShapeDtypeStruct((nt, batch, 2 * D_MODEL), F32),
            jax.ShapeDtypeStruct((POOL_BUF, batch, D_POOL), F32),
            jax.ShapeDtypeStruct((CONV_WIDTH - 1, batch, CONV_DIM), F32),
            jax.ShapeDtypeStruct((batch, D_SSM, D_STATE), F32),
        ],
        compiler_params=pltpu.CompilerParams(
            dimension_semantics=("arbitrary",), vmem_limit_bytes=VMEM_LIMIT),
        name="sample_mix",
    )(u, z, xbc, dt, pbuf, cbuf, state, wpool, pscale, convw, convb, dtb, alog, dskip_x, gssm, expand)


def _mix_constants(w_pool, pool_scale, conv_w, conv_b, dt_bias, a_log, d_skip, ssm_norm_g):
    pad = DT_PAD - SSM_HEADS
    head_of_channel = jnp.arange(D_SSM, dtype=jnp.int32) // HEAD_DIM
    slot_row = jnp.arange(DT_PAD, dtype=jnp.int32)
    expand = ((slot_row[:, None] % HEAD_SLOT == head_of_channel[None, :])
              & (slot_row[:, None] < 3 * HEAD_SLOT)).astype(BF16)
    tri = (jnp.arange(CHUNK)[:, None] >= jnp.arange(CHUNK)[None, :]).astype(BF16)
    return (
        w_pool.astype(BF16),
        pool_scale.reshape(1, D_POOL),
        conv_w,
        conv_b.reshape(1, CONV_DIM),
        jnp.pad(dt_bias, (0, pad)).reshape(1, DT_PAD),
        jnp.pad(a_log, (0, pad)).reshape(1, DT_PAD),
        jnp.repeat(d_skip, HEAD_DIM).reshape(1, D_SSM),
        ssm_norm_g.reshape(1, D_SSM),
        tri,
        expand,
    )


PROMPT_IN_TM = 512
PROMPT_FFN_TM = 256
SAMPLE_TM = 256


def kernel(x_prompt, x_sample, p_prompt, p_sample, state_pool, state_conv, state_ssm, w_in, w_pool, pool_scale, conv_w, conv_b, dt_bias, a_log, d_skip, ssm_norm_g, w_out, norm_mix_g, norm_mlp_g, w_ff1, w_ff2, norm_ple_g, w_gate, w_ple, final_norm_g):
    assert w_in.shape[0] == 1, "single-layer trunk"
    bp, tp = x_prompt.shape[:2]
    bs, ts = x_sample.shape[:2]
    assert tp % CHUNK == 0 and tp >= POOL_BUF and ts == SAMPLE_T and bs % SAMPLE_BB == 0

    row = lambda v: v.reshape(1, -1)
    w_in_pad = jnp.pad(w_in[0], ((0, 0), (0, D_IN_PAD - w_in.shape[2]))).astype(BF16)
    consts = _mix_constants(w_pool[0], pool_scale[0], conv_w[0], conv_b[0], dt_bias[0], a_log[0],
                            d_skip[0], ssm_norm_g[0])
    ffn_weights = (w_out[0].astype(BF16), row(norm_mlp_g[0]), w_ff1[0].astype(BF16),
                   w_ff2[0].astype(BF16), row(norm_ple_g[0]), w_gate[0].astype(BF16),
                   w_ple[0].astype(BF16), row(final_norm_g))
    g_mix = row(norm_mix_g[0])

    xp = x_prompt.reshape(bp * tp, D_MODEL)
    u, z, xbc, dt = _in_proj(xp, g_mix, w_in_pad, PROMPT_IN_TM)
    mix, ssm_p = _prompt_mix(u, z, xbc, dt, consts, bp, tp)
    y_prompt = _out_ffn(xp, mix, p_prompt[0].reshape(bp * tp, D_PLE), *ffn_weights, PROMPT_FFN_TM)
    pool_p = u.reshape(bp, tp, D_POOL)[:, tp - POOL_BUF:, :]
    conv_p = xbc.reshape(bp, tp, CONV_DIM)[:, tp - (CONV_WIDTH - 1):, :]

    tmaj = lambda a: jnp.transpose(a, (1, 0, 2))
    xs_t = tmaj(x_sample).reshape(ts * bs, D_MODEL)
    u_s, z_s, xbc_s, dt_s = _in_proj(xs_t, g_mix, w_in_pad, SAMPLE_TM)
    unflat = lambda a: a.reshape(ts, bs, a.shape[-1])
    mix_s, pool_s, conv_s, ssm_s = _sample_group_mix(
        unflat(u_s), unflat(z_s), unflat(xbc_s), unflat(dt_s), tmaj(state_pool[0]), tmaj(state_conv[0]),
        state_ssm[0].reshape(bs, D_SSM, D_STATE), consts)
    y_s = _out_ffn(xs_t, mix_s.reshape(ts * bs, 2 * D_MODEL), tmaj(p_sample[0]).reshape(ts * bs, D_PLE),
                   *ffn_weights, SAMPLE_TM)
    y_sample = tmaj(y_s.reshape(ts, bs, D_MODEL))

    state_shape = (1, -1, SSM_HEADS, HEAD_DIM, D_STATE)
    return (y_prompt.reshape(bp, tp, D_MODEL), y_sample,
            pool_p[None], conv_p[None], ssm_p.reshape(state_shape),
            tmaj(pool_s)[None], tmaj(conv_s)[None], ssm_s.reshape(state_shape))
```

```python
import functools
import math

import jax
import jax.numpy as jnp
from jax import lax
from jax.experimental import pallas as pl
from jax.experimental.pallas import tpu as pltpu

F32 = jnp.float32
BF16 = jnp.bfloat16

D_MODEL = 1024
D_POOL = 1024
D_SSM = 1024
POOL_WINDOWS = (2, 4, 8, 16)
POOL_GROUP = 256
POOL_BUF = 15
SSM_HEADS = 16
HEAD_DIM = 64
SSM_GROUPS = 2
HEADS_PER_GROUP = 8
GROUP_WIDTH = HEADS_PER_GROUP * HEAD_DIM
D_STATE = 128
CONV_WIDTH = 4
CONV_DIM = D_SSM + 2 * SSM_GROUPS * D_STATE
CHUNK = 128
D_FF = 4096
D_PLE = 256
PAST_LEN = 16384
EPS = 1e-6

LANES = 128
SUBLANES = 8
DT_PAD = LANES
HEAD_SLOT = SSM_HEADS
D_IN_PAD = D_POOL + D_SSM + CONV_DIM + DT_PAD
POOL_HIST = 16
CONV_HIST = SUBLANES
VMEM_LIMIT = 56 * 1024 * 1024
FFN_STAGES = 4


def _rms(x, g):
    return x * lax.rsqrt(jnp.mean(x * x, axis=-1, keepdims=True) + EPS) * g


def _silu(x):
    return x * jax.nn.sigmoid(x)


def _softplus(x):
    return jnp.maximum(x, 0.0) + jnp.log1p(jnp.exp(-jnp.abs(x)))


def _neg_exp_heads(a_log_row):
    lane = lax.broadcasted_iota(jnp.int32, a_log_row.shape, a_log_row.ndim - 1)
    return jnp.where(lane < SSM_HEADS, -jnp.exp(a_log_row), 0.0)


def _dot(a, b):
    return jnp.dot(a.astype(BF16), b.astype(BF16), preferred_element_type=F32)


def _split3(x):
    hi = x.astype(BF16)
    r1 = x - hi.astype(F32)
    mid = r1.astype(BF16)
    lo = (r1 - mid.astype(F32)).astype(BF16)
    return hi, mid, lo


def _pack_heads(v):
    lane = lax.broadcasted_iota(jnp.int32, v.shape, 1)
    hi, mid, lo = _split3(jnp.where(lane < SSM_HEADS, v, 0.0))
    packed = (hi.astype(F32) + pltpu.roll(mid.astype(F32), HEAD_SLOT, axis=1)
              + pltpu.roll(lo.astype(F32), 2 * HEAD_SLOT, axis=1))
    return packed.astype(BF16)


def _expand_heads(v, expand3):
    return jnp.dot(_pack_heads(v), expand3, preferred_element_type=F32)


def _chunk_cumsum(tri16, a):
    s = jnp.dot(tri16, _pack_heads(a), preferred_element_type=F32)
    return s + pltpu.roll(s, DT_PAD - HEAD_SLOT, axis=1) + pltpu.roll(s, DT_PAD - 2 * HEAD_SLOT, axis=1)


def _in_proj_kernel(x_ref, g_ref, w_ref, u_ref, z_ref, xbc_ref, dt_ref):
    xn = _rms(x_ref[...], g_ref[...]).astype(BF16)
    u_ref[...] = jnp.dot(xn, w_ref[:, 0:D_POOL], preferred_element_type=F32)
    z_ref[...] = jnp.dot(xn, w_ref[:, D_POOL:D_POOL + D_SSM], preferred_element_type=F32)
    xbc_ref[...] = jnp.dot(xn, w_ref[:, D_POOL + D_SSM:D_POOL + D_SSM + CONV_DIM],
                           preferred_element_type=F32)
    dt_ref[...] = jnp.dot(xn, w_ref[:, D_POOL + D_SSM + CONV_DIM:D_IN_PAD],
                          preferred_element_type=F32)


def _in_proj(x2d, g, w_in_pad, tm):
    n = x2d.shape[0]
    row = lambda i: (i, 0)
    fixed = lambda i: (0, 0)
    return pl.pallas_call(
        _in_proj_kernel,
        grid=(n // tm,),
        in_specs=[
            pl.BlockSpec((tm, D_MODEL), row),
            pl.BlockSpec((1, D_MODEL), fixed),
            pl.BlockSpec((D_MODEL, D_IN_PAD), fixed),
        ],
        out_specs=[
            pl.BlockSpec((tm, D_POOL), row),
            pl.BlockSpec((tm, D_SSM), row),
            pl.BlockSpec((tm, CONV_DIM), row),
            pl.BlockSpec((tm, DT_PAD), row),
        ],
        out_shape=[
            jax.ShapeDtypeStruct((n, D_POOL), F32),
            jax.ShapeDtypeStruct((n, D_SSM), F32),
            jax.ShapeDtypeStruct((n, CONV_DIM), F32),
            jax.ShapeDtypeStruct((n, DT_PAD), F32),
        ],
        compiler_params=pltpu.CompilerParams(
            dimension_semantics=("arbitrary",), vmem_limit_bytes=VMEM_LIMIT),
        name="in_proj",
    )(x2d, g, w_in_pad)


def _run(steps):
    try:
        while True:
            next(steps)
    except StopIteration as done:
        return done.value


def _interleave(main, side, side_per_main):
    result = None
    main_done = side_done = False
    while not (main_done and side_done):
        if not main_done:
            try:
                next(main)
            except StopIteration as done:
                result, main_done = done.value, True
        for _ in range(side_per_main):
            if side_done:
                break
            try:
                next(side)
            except StopIteration:
                side_done = True
    return result


def _ffn_tile_steps(x, mix16, p, wout_ref, g_mlp_ref, w1_ref, w2_ref, g_ple_ref, wg_ref, wple_ref, g_fin_ref):
    h = x + jnp.dot(mix16, wout_ref[...], preferred_element_type=F32)
    yield
    hn = _rms(h, g_mlp_ref[...]).astype(BF16)
    ff_block = D_FF // FFN_STAGES
    for c in range(FFN_STAGES):
        f = jnp.dot(hn, w1_ref[:, c * ff_block:(c + 1) * ff_block], preferred_element_type=F32)
        f = jnp.square(jnp.maximum(f, 0.0)).astype(BF16)
        h = h + jnp.dot(f, w2_ref[c * ff_block:(c + 1) * ff_block, :], preferred_element_type=F32)
        yield
    gate = jax.nn.sigmoid(
        jnp.dot(_rms(h, g_ple_ref[...]).astype(BF16), wg_ref[...], preferred_element_type=F32))
    h = h + gate * jnp.dot(p.astype(BF16), wple_ref[...], preferred_element_type=F32)
    return _rms(h, g_fin_ref[...])


def _out_ffn_kernel(x_ref, mix_ref, p_ref, *refs):
    *weight_refs, y_ref = refs
    y_ref[...] = _run(_ffn_tile_steps(x_ref[...], mix_ref[...].astype(BF16), p_ref[...], *weight_refs))


def _out_ffn(x2d, mix2d, p2d, wout, g_mlp, w1, w2, g_ple, wg, wple, g_fin, tm):
    n = x2d.shape[0]
    row = lambda i: (i, 0)
    fixed = lambda i: (0, 0)

    def resident(shape):
        return pl.BlockSpec(shape, fixed, pipeline_mode=pl.Buffered(1))

    return pl.pallas_call(
        _out_ffn_kernel,
        grid=(n // tm,),
        in_specs=[
            pl.BlockSpec((tm, D_MODEL), row),
            pl.BlockSpec((tm, 2 * D_MODEL), row),
            pl.BlockSpec((tm, D_PLE), row),
            resident((2 * D_MODEL, D_MODEL)),
            resident((1, D_MODEL)),
            resident((D_MODEL, D_FF)),
            resident((D_FF, D_MODEL)),
            resident((1, D_MODEL)),
            resident((D_MODEL, D_MODEL)),
            resident((D_PLE, D_MODEL)),
            resident((1, D_MODEL)),
        ],
        out_specs=pl.BlockSpec((tm, D_MODEL), row),
        out_shape=jax.ShapeDtypeStruct((n, D_MODEL), F32),
        compiler_params=pltpu.CompilerParams(
            dimension_semantics=("arbitrary",), vmem_limit_bytes=VMEM_LIMIT),
        name="out_ffn",
    )(x2d, mix2d, p2d, wout, g_mlp, w1, w2, g_ple, wg, wple, g_fin)


PROMPT_TILE = 256


def _mix_chunk_steps(u, z, xbc, dt_raw, pos0, keep, const_refs, pool_hist, conv_hist, state_t):
    (wpool_ref, pscale_ref, convw_ref, convb_ref, dtb_ref, alog_ref, dskip_ref, gssm_ref, tri_ref,
     expand_ref) = const_refs
    carried = (lambda v: v) if keep is None else (lambda v: v * keep)

    ext = jnp.concatenate([carried(pool_hist[...]), u], axis=0)
    pool_hist[...] = u[CHUNK - POOL_HIST:, :]
    pos = pos0 + lax.broadcasted_iota(jnp.int32, (CHUNK, 1), 0)
    y_pool = []
    for g, w in enumerate(POOL_WINDOWS):
        cols = slice(g * POOL_GROUP, (g + 1) * POOL_GROUP)
        s = ext[:, cols]
        shift = 1
        while shift < w:
            s = s + pltpu.roll(s, shift, axis=0)
            shift *= 2
        count = jnp.minimum(w, pos + 1).astype(F32)
        d = s[POOL_HIST:, :] / count - u[:, cols]
        y_pool.append(_dot(d, wpool_ref[g]))
    y_pool = jnp.concatenate(y_pool, axis=1) * pscale_ref[...]
    yield

    extc = jnp.concatenate([carried(conv_hist[...]), xbc], axis=0)
    conv_hist[...] = xbc[CHUNK - CONV_HIST:, :]
    acc = convb_ref[...]
    for k in range(CONV_WIDTH):
        back = CONV_WIDTH - 1 - k
        tap = extc if back == 0 else pltpu.roll(extc, back, axis=0)
        acc = acc + tap[CONV_HIST:, :] * convw_ref[k:k + 1, :]
    xc = _silu(acc)
    xs = xc[:, 0:D_SSM]
    b_all = xc[:, D_SSM:D_SSM + SSM_GROUPS * D_STATE]
    c_all = xc[:, D_SSM + SSM_GROUPS * D_STATE:CONV_DIM]
    yield

    dt = _softplus(dt_raw + dtb_ref[...])
    a_cum = _chunk_cumsum(tri_ref[...], dt * _neg_exp_heads(alog_ref[...]))
    a_cum_t = a_cum.T
    a_last = a_cum[CHUNK - 1:CHUNK, :]
    expand = expand_ref[...]
    dt_x = _expand_heads(dt, expand)
    ea_x = _expand_heads(jnp.exp(a_cum), expand)
    de_x = _expand_heads(jnp.exp(a_last - a_cum), expand)
    cd_x = _expand_heads(jnp.broadcast_to(jnp.exp(a_last), (2 * SUBLANES, DT_PAD)), expand)[0:1, :]
    x_dt = xs * dt_x
    x_w = x_dt * de_x
    yield

    row_i = lax.broadcasted_iota(jnp.int32, (CHUNK, CHUNK), 0)
    col_i = lax.broadcasted_iota(jnp.int32, (CHUNK, CHUNK), 1)
    causal = row_i >= col_i
    lane = lax.broadcasted_iota(jnp.int32, (CHUNK, 2 * HEAD_DIM), 1)
    first_head = lane < HEAD_DIM

    y_parts = []
    for g in range(SSM_GROUPS):
        gch = slice(g * GROUP_WIDTH, (g + 1) * GROUP_WIDTH)
        b_g = b_all[:, g * D_STATE:(g + 1) * D_STATE].astype(BF16)
        c_g = c_all[:, g * D_STATE:(g + 1) * D_STATE].astype(BF16)
        cb = lax.dot_general(c_g, b_g, (((1,), (1,)), ((), ())), preferred_element_type=F32)
        st_g = carried(state_t[:, gch])
        y_off = jnp.dot(c_g, st_g.astype(BF16), preferred_element_type=F32) * ea_x[:, gch]
        y_diag = []
        for j in range(HEADS_PER_GROUP // 2):
            h0 = g * HEADS_PER_GROUP + 2 * j
            m = []
            for h in (h0, h0 + 1):
                seg = a_cum[:, h:h + 1] - a_cum_t[h:h + 1, :]
                m.append((cb * jnp.exp(jnp.where(causal, seg, -jnp.inf))).astype(BF16))
            pair = x_dt[:, h0 * HEAD_DIM:(h0 + 2) * HEAD_DIM]
            rhs = jnp.concatenate([jnp.where(first_head, pair, 0.0),
                                   jnp.where(first_head, 0.0, pair)], axis=0).astype(BF16)
            y_diag.append(jnp.dot(jnp.concatenate(m, axis=1), rhs, preferred_element_type=F32))
            yield
        y_parts.append(jnp.concatenate(y_diag, axis=1) + y_off)
        upd = lax.dot_general(b_g, x_w[:, gch].astype(BF16), (((0,), (0,)), ((), ())),
                              preferred_element_type=F32)
        state_t[:, gch] = st_g * cd_x[:, gch] + upd

    y = jnp.concatenate(y_parts, axis=1) + dskip_ref[...] * xs
    y = y * _silu(z)
    return y_pool, _rms(y, gssm_ref[...])


N_MIX_CONSTS = 10
N_FFN_WEIGHTS = 8
MIXER_STAGES_PER_FFN_STAGE = 4


def _prompt_mix_ffn_kernel(u_ref, z_ref, xbc_ref, dt_ref, x_ref, p_ref, *refs, n_tiles, tiles_per_seq):
    const_refs = refs[:N_MIX_CONSTS]
    weight_refs = refs[N_MIX_CONSTS:N_MIX_CONSTS + N_FFN_WEIGHTS]
    y_ref, ssm_ref, mix_scr, pool_hist, conv_hist, state_t = refs[N_MIX_CONSTS + N_FFN_WEIGHTS:]
    s = pl.program_id(0)

    @pl.when(s == 0)
    def _():
        mix_scr[...] = jnp.zeros_like(mix_scr)
        pool_hist[...] = jnp.zeros_like(pool_hist)
        conv_hist[...] = jnp.zeros_like(conv_hist)
        state_t[...] = jnp.zeros_like(state_t)

    tile_in_seq = jnp.minimum(s, n_tiles - 1) % tiles_per_seq
    keep = jnp.where(tile_in_seq == 0, 0.0, 1.0)
    slot = s % 2

    def mixer_steps():
        for ci in range(PROMPT_TILE // CHUNK):
            rows = slice(ci * CHUNK, (ci + 1) * CHUNK)
            y_pool, y_ssm = yield from _mix_chunk_steps(
                u_ref[rows, :], z_ref[rows, :], xbc_ref[rows, :], dt_ref[rows, :],
                tile_in_seq * PROMPT_TILE + ci * CHUNK, keep if ci == 0 else None,
                const_refs, pool_hist, conv_hist, state_t)
            mix_scr[slot, rows, 0:D_POOL] = y_pool.astype(BF16)
            mix_scr[slot, rows, D_POOL:] = y_ssm.astype(BF16)
            yield

    ffn_steps = _ffn_tile_steps(x_ref[...], mix_scr[(s + 1) % 2], p_ref[...], *weight_refs)
    y_ref[...] = _interleave(ffn_steps, mixer_steps(), MIXER_STAGES_PER_FFN_STAGE)

    @pl.when((tile_in_seq == tiles_per_seq - 1) & (s < n_tiles))
    def _():
        for j in range(D_SSM // LANES):
            ssm_ref[j * LANES:(j + 1) * LANES, :] = state_t[:, j * LANES:(j + 1) * LANES].T


def _prompt_mix_ffn(u, z, xbc, dt, x2d, p2d, consts, ffn_weights, batch, seq):
    tiles_per_seq = seq // PROMPT_TILE
    n_tiles = batch * tiles_per_seq
    mix_row = lambda s: (jnp.minimum(s, n_tiles - 1), 0)
    ffn_row = lambda s: (jnp.maximum(s - 1, 0), 0)
    fixed2 = lambda s: (0, 0)
    fixed3 = lambda s: (0, 0, 0)

    def resident(a):
        return pl.BlockSpec(a.shape, fixed3 if a.ndim == 3 else fixed2, pipeline_mode=pl.Buffered(1))

    return pl.pallas_call(
        functools.partial(_prompt_mix_ffn_kernel, n_tiles=n_tiles, tiles_per_seq=tiles_per_seq),
        grid=(n_tiles + 1,),
        in_specs=[
            pl.BlockSpec((PROMPT_TILE, D_POOL), mix_row),
            pl.BlockSpec((PROMPT_TILE, D_SSM), mix_row),
            pl.BlockSpec((PROMPT_TILE, CONV_DIM), mix_row),
            pl.BlockSpec((PROMPT_TILE, DT_PAD), mix_row),
            pl.BlockSpec((PROMPT_TILE, D_MODEL), ffn_row),
            pl.BlockSpec((PROMPT_TILE, D_PLE), ffn_row),
        ] + [resident(a) for a in consts] + [resident(a) for a in ffn_weights],
        out_specs=[
            pl.BlockSpec((PROMPT_TILE, D_MODEL), ffn_row),
            pl.BlockSpec((None, D_SSM, D_STATE),
                         lambda s: (jnp.minimum(s, n_tiles - 1) // tiles_per_seq, 0, 0)),
        ],
        out_shape=[
            jax.ShapeDtypeStruct((batch * seq, D_MODEL), F32),
            jax.ShapeDtypeStruct((batch, D_SSM, D_STATE), F32),
        ],
        scratch_shapes=[
            pltpu.VMEM((2, PROMPT_TILE, 2 * D_MODEL), BF16),
            pltpu.VMEM((POOL_HIST, D_POOL), F32),
            pltpu.VMEM((CONV_HIST, CONV_DIM), F32),
            pltpu.VMEM((D_STATE, D_SSM), F32),
        ],
        compiler_params=pltpu.CompilerParams(
            dimension_semantics=("arbitrary",), vmem_limit_bytes=VMEM_LIMIT),
        name="prompt_mix_ffn",
    )(u, z, xbc, dt, x2d, p2d, *consts, *ffn_weights)


SAMPLE_T = 4
SAMPLE_BB = 8


def _sample_mix_kernel(u_ref, z_ref, xbc_ref, dt_ref, pbuf_ref, cbuf_ref, state_ref,
                       wpool_ref, pscale_ref, convw_ref, convb_ref, dtb_ref, alog_ref, dskip_ref,
                       gssm_ref, expand_ref,
                       mix_ref, npool_ref, nconv_ref, nstate_ref):
    nt, bb = SAMPLE_T, SAMPLE_BB
    rows = nt * bb

    ext = [pbuf_ref[i] for i in range(POOL_BUF)] + [u_ref[t] for t in range(nt)]
    for i in range(POOL_BUF):
        npool_ref[i] = ext[i + nt]
    y_pool = []
    for g, w in enumerate(POOL_WINDOWS):
        cols = slice(g * POOL_GROUP, (g + 1) * POOL_GROUP)
        d = []
        for t in range(nt):
            hi = POOL_BUF + t
            win = ext[hi - w + 1][:, cols]
            for i in range(hi - w + 2, hi + 1):
                win = win + ext[i][:, cols]
            count = float(min(w, PAST_LEN + t + 1))
            d.append(win / count - ext[hi][:, cols])
        y_pool.append(_dot(jnp.concatenate(d, axis=0), wpool_ref[g]))
    y_pool = jnp.concatenate(y_pool, axis=1) * pscale_ref[...]

    extc = [cbuf_ref[i] for i in range(CONV_WIDTH - 1)] + [xbc_ref[t] for t in range(nt)]
    for j in range(CONV_WIDTH - 1):
        nconv_ref[j] = extc[nt + j]
    xc = []
    for t in range(nt):
        acc = convb_ref[...]
        for k in range(CONV_WIDTH):
            acc = acc + extc[t + k] * convw_ref[k:k + 1, :]
        xc.append(_silu(acc))
    xs = [v[:, 0:D_SSM] for v in xc]
    bm = [v[:, D_SSM:D_SSM + SSM_GROUPS * D_STATE] for v in xc]
    cm = [v[:, D_SSM + SSM_GROUPS * D_STATE:CONV_DIM] for v in xc]

    a_row = _neg_exp_heads(alog_ref[...])
    dt = [_softplus(dt_ref[t] + dtb_ref[...]) for t in range(nt)]
    a_cum = []
    for t in range(nt):
        a_t = dt[t] * a_row
        a_cum.append(a_t if t == 0 else a_cum[-1] + a_t)
    lane = lax.broadcasted_iota(jnp.int32, (bb, DT_PAD), 1)
    in_group0 = lane < HEADS_PER_GROUP
    head_rows = []
    pair_index = {}
    for l in range(nt):
        for s in range(l + 1):
            cb = [jnp.sum(cm[l][:, g * D_STATE:(g + 1) * D_STATE] * bm[s][:, g * D_STATE:(g + 1) * D_STATE],
                          axis=-1, keepdims=True) for g in range(SSM_GROUPS)]
            cb_heads = jnp.where(in_group0, cb[0], cb[1])
            pair_index[(l, s)] = len(head_rows)
            head_rows.append(cb_heads * jnp.exp(a_cum[l] - a_cum[s]) * dt[s])
    ea_index = len(head_rows)
    head_rows += [jnp.exp(a_cum[t]) for t in range(nt)]
    w_index = len(head_rows)
    head_rows += [jnp.exp(a_cum[nt - 1] - a_cum[t]) * dt[t] for t in range(nt)]
    cd_index = len(head_rows)
    head_rows.append(jnp.exp(a_cum[nt - 1]))
    chan = _expand_heads(jnp.concatenate(head_rows, axis=0), expand_ref[...])
    chan_rows = lambda i: chan[i * bb:(i + 1) * bb, :]

    y_diag = []
    for l in range(nt):
        acc = chan_rows(pair_index[(l, 0)]) * xs[0]
        for s in range(1, l + 1):
            acc = acc + chan_rows(pair_index[(l, s)]) * xs[s]
        y_diag.append(acc)
    xs_all = jnp.concatenate(xs, axis=0)
    xw_all = jnp.concatenate([chan_rows(w_index + t) * xs[t] for t in range(nt)], axis=0)
    ea_all = jnp.concatenate([chan_rows(ea_index + t) for t in range(nt)], axis=0)
    b_all = jnp.concatenate(bm, axis=0).astype(BF16)
    c_all = jnp.concatenate(cm, axis=0).astype(BF16)
    cd3 = jnp.concatenate([p.astype(F32) for p in _split3(chan_rows(cd_index))], axis=0)
    ones3 = jnp.ones((3 * bb, D_STATE), BF16)
    row_batch = lax.broadcasted_iota(jnp.int32, (rows, 1), 0) & (bb - 1)
    row_batch3 = lax.broadcasted_iota(jnp.int32, (3 * bb, 1), 0) & (bb - 1)
    contract_rows = (((0,), (0,)), ((), ()))

    def per_batch(b, y_off):
        st = state_ref[b]
        st16 = st.astype(BF16)
        mine = row_batch == b
        xw_b = jnp.where(mine, xw_all, 0.0).astype(BF16)
        cd_b = jnp.where(row_batch3 == b, cd3, 0.0).astype(BF16)
        scale = lax.dot_general(cd_b, ones3, contract_rows, preferred_element_type=F32)
        yo = []
        for g in range(SSM_GROUPS):
            grow = slice(g * GROUP_WIDTH, (g + 1) * GROUP_WIDTH)
            gst = slice(g * D_STATE, (g + 1) * D_STATE)
            yo.append(lax.dot_general(c_all[:, gst], st16[grow, :], (((1,), (1,)), ((), ())),
                                      preferred_element_type=F32))
            upd = lax.dot_general(xw_b[:, grow], b_all[:, gst], contract_rows,
                                  preferred_element_type=F32)
            nstate_ref[b, grow, :] = st[grow, :] * scale[grow, :] + upd
        return jnp.where(mine, jnp.concatenate(yo, axis=1), y_off)

    y_off = lax.fori_loop(0, bb, per_batch, jnp.zeros((rows, D_SSM), F32))
    y = jnp.concatenate(y_diag, axis=0) + y_off * ea_all + dskip_ref[...] * xs_all
    y = y * _silu(jnp.concatenate([z_ref[t] for t in range(nt)], axis=0))
    y_ssm = _rms(y, gssm_ref[...])
    for t in range(nt):
        mix_ref[t, :, 0:D_POOL] = y_pool[t * bb:(t + 1) * bb, :].astype(mix_ref.dtype)
        mix_ref[t, :, D_POOL:] = y_ssm[t * bb:(t + 1) * bb, :].astype(mix_ref.dtype)


def _sample_group_mix(u, z, xbc, dt, pbuf, cbuf, state, consts):
    nt, bb = SAMPLE_T, SAMPLE_BB
    batch = u.shape[1]
    (wpool, pscale, convw, convb, dtb, alog, dskip_x, gssm, _, expand) = consts
    blk = lambda i: (0, i, 0)
    fixed2 = lambda i: (0, 0)
    fixed3 = lambda i: (0, 0, 0)
    return pl.pallas_call(
        _sample_mix_kernel,
        grid=(batch // bb,),
        in_specs=[
            pl.BlockSpec((nt, bb, D_POOL), blk),
            pl.BlockSpec((nt, bb, D_SSM), blk),
            pl.BlockSpec((nt, bb, CONV_DIM), blk),
            pl.BlockSpec((nt, bb, DT_PAD), blk),
            pl.BlockSpec((POOL_BUF, bb, D_POOL), blk),
            pl.BlockSpec((CONV_WIDTH - 1, bb, CONV_DIM), blk),
            pl.BlockSpec((bb, D_SSM, D_STATE), lambda i: (i, 0, 0)),
            pl.BlockSpec((len(POOL_WINDOWS), POOL_GROUP, POOL_GROUP), fixed3),
            pl.BlockSpec((1, D_POOL), fixed2),
            pl.BlockSpec((CONV_WIDTH, CONV_DIM), fixed2),
            pl.BlockSpec((1, CONV_DIM), fixed2),
            pl.BlockSpec((1, DT_PAD), fixed2),
            pl.BlockSpec((1, DT_PAD), fixed2),
            pl.BlockSpec((1, D_SSM), fixed2),
            pl.BlockSpec((1, D_SSM), fixed2),
            pl.BlockSpec((DT_PAD, D_SSM), fixed2),
        ],
        out_specs=[
            pl.BlockSpec((nt, bb, 2 * D_MODEL), blk),
            pl.BlockSpec((POOL_BUF, bb, D_POOL), blk),
            pl.BlockSpec((CONV_WIDTH - 1, bb, CONV_DIM), blk),
            pl.BlockSpec((bb, D_SSM, D_STATE), lambda i: (i, 0, 0)),
        ],
        out_shape=[
            jax.ShapeDtypeStruct((nt, batch, 2 * D_MODEL), F32),
            jax.ShapeDtypeStruct((POOL_BUF, batch, D_POOL), F32),
            jax.ShapeDtypeStruct((CONV_WIDTH - 1, batch, CONV_DIM), F32),
            jax.ShapeDtypeStruct((batch, D_SSM, D_STATE), F32),
        ],
        compiler_params=pltpu.CompilerParams(
            dimension_semantics=("arbitrary",), vmem_limit_bytes=VMEM_LIMIT),
        name="sample_mix",
    )(u, z, xbc, dt, pbuf, cbuf, state, wpool, pscale, convw, convb, dtb, alog, dskip_x, gssm, expand)


def _mix_constants(w_pool, pool_scale, conv_w, conv_b, dt_bias, a_log, d_skip, ssm_norm_g):
    pad = DT_PAD - SSM_HEADS
    head_of_channel = jnp.arange(D_SSM, dtype=jnp.int32) // HEAD_DIM
    slot_row = jnp.arange(DT_PAD, dtype=jnp.int32)
    expand = ((slot_row[:, None] % HEAD_SLOT == head_of_channel[None, :])
              & (slot_row[:, None] < 3 * HEAD_SLOT)).astype(BF16)
    tri = (jnp.arange(CHUNK)[:, None] >= jnp.arange(CHUNK)[None, :]).astype(BF16)
    return (
        w_pool.astype(BF16),
        pool_scale.reshape(1, D_POOL),
        conv_w,
        conv_b.reshape(1, CONV_DIM),
        jnp.pad(dt_bias, (0, pad)).reshape(1, DT_PAD),
        jnp.pad(a_log, (0, pad)).reshape(1, DT_PAD),
        jnp.repeat(d_skip, HEAD_DIM).reshape(1, D_SSM),
        ssm_norm_g.reshape(1, D_SSM),
        tri,
        expand,
    )


PROMPT_IN_TM = 512
SAMPLE_TM = 256


def kernel(x_prompt, x_sample, p_prompt, p_sample, state_pool, state_conv, state_ssm, w_in, w_pool, pool_scale, conv_w, conv_b, dt_bias, a_log, d_skip, ssm_norm_g, w_out, norm_mix_g, norm_mlp_g, w_ff1, w_ff2, norm_ple_g, w_gate, w_ple, final_norm_g):
    assert w_in.shape[0] == 1, "single-layer trunk"
    bp, tp = x_prompt.shape[:2]
    bs, ts = x_sample.shape[:2]
    assert tp % PROMPT_TILE == 0 and tp >= POOL_BUF and ts == SAMPLE_T and bs % SAMPLE_BB == 0

    row = lambda v: v.reshape(1, -1)
    w_in_pad = jnp.pad(w_in[0], ((0, 0), (0, D_IN_PAD - w_in.shape[2]))).astype(BF16)
    consts = _mix_constants(w_pool[0], pool_scale[0], conv_w[0], conv_b[0], dt_bias[0], a_log[0],
                            d_skip[0], ssm_norm_g[0])
    ffn_weights = (w_out[0].astype(BF16), row(norm_mlp_g[0]), w_ff1[0].astype(BF16),
                   w_ff2[0].astype(BF16), row(norm_ple_g[0]), w_gate[0].astype(BF16),
                   w_ple[0].astype(BF16), row(final_norm_g))
    g_mix = row(norm_mix_g[0])

    xp = x_prompt.reshape(bp * tp, D_MODEL)
    u, z, xbc, dt = _in_proj(xp, g_mix, w_in_pad, PROMPT_IN_TM)
    y_prompt, ssm_p = _prompt_mix_ffn(u, z, xbc, dt, xp, p_prompt[0].reshape(bp * tp, D_PLE), consts,
                                      ffn_weights, bp, tp)
    pool_p = u.reshape(bp, tp, D_POOL)[:, tp - POOL_BUF:, :]
    conv_p = xbc.reshape(bp, tp, CONV_DIM)[:, tp - (CONV_WIDTH - 1):, :]

    tmaj = lambda a: jnp.transpose(a, (1, 0, 2))
    xs_t = tmaj(x_sample).reshape(ts * bs, D_MODEL)
    u_s, z_s, xbc_s, dt_s = _in_proj(xs_t, g_mix, w_in_pad, SAMPLE_TM)
    unflat = lambda a: a.reshape(ts, bs, a.shape[-1])
    mix_s, pool_s, conv_s, ssm_s = _sample_group_mix(
        unflat(u_s), unflat(z_s), unflat(xbc_s), unflat(dt_s), tmaj(state_pool[0]), tmaj(state_conv[0]),
        state_ssm[0].reshape(bs, D_SSM, D_STATE), consts)
    y_s = _out_ffn(xs_t, mix_s.reshape(ts * bs, 2 * D_MODEL), tmaj(p_sample[0]).reshape(ts * bs, D_PLE),
                   *ffn_weights, SAMPLE_TM)
    y_sample = tmaj(y_s.reshape(ts, bs, D_MODEL))

    state_shape = (1, -1, SSM_HEADS, HEAD_DIM, D_STATE)
    return (y_prompt.reshape(bp, tp, D_MODEL), y_sample,
            pool_p[None], conv_p[None], ssm_p.reshape(state_shape),
            tmaj(pool_s)[None], tmaj(conv_s)[None], ssm_s.reshape(state_shape))
```

```python
import functools
import math

import jax
import jax.numpy as jnp
from jax import lax
from jax.experimental import pallas as pl
from jax.experimental.pallas import tpu as pltpu

F32 = jnp.float32
BF16 = jnp.bfloat16

D_MODEL = 1024
D_POOL = 1024
D_SSM = 1024
POOL_WINDOWS = (2, 4, 8, 16)
POOL_GROUP = 256
POOL_BUF = 15
SSM_HEADS = 16
HEAD_DIM = 64
SSM_GROUPS = 2
HEADS_PER_GROUP = 8
GROUP_WIDTH = HEADS_PER_GROUP * HEAD_DIM
D_STATE = 128
CONV_WIDTH = 4
CONV_DIM = D_SSM + 2 * SSM_GROUPS * D_STATE
CHUNK = 128
D_FF = 4096
D_PLE = 256
PAST_LEN = 16384
EPS = 1e-6

LANES = 128
SUBLANES = 8
DT_PAD = LANES
HEAD_SLOT = SSM_HEADS
D_IN_PAD = D_POOL + D_SSM + CONV_DIM + DT_PAD
POOL_HIST = 16
CONV_HIST = SUBLANES
VMEM_LIMIT = 56 * 1024 * 1024
FFN_STAGES = 4
FFN_COST_OUT_PROJ = 2048
FFN_COST_FF_HALF = 1024
FFN_COST_TOTAL = FFN_COST_OUT_PROJ + 2 * FFN_STAGES * FFN_COST_FF_HALF + 1280
MIX_COST_POOL = 350
MIX_COST_CONV = 900
MIX_COST_PREP = 450
MIX_COST_HEAD_PAIR = 230
MIX_COST_TAIL = 350


def _rms(x, g):
    return x * lax.rsqrt(jnp.mean(x * x, axis=-1, keepdims=True) + EPS) * g


def _silu(x):
    return x * jax.nn.sigmoid(x)


def _softplus(x):
    return jnp.maximum(x, 0.0) + jnp.log1p(jnp.exp(-jnp.abs(x)))


def _neg_exp_heads(a_log_row):
    lane = lax.broadcasted_iota(jnp.int32, a_log_row.shape, a_log_row.ndim - 1)
    return jnp.where(lane < SSM_HEADS, -jnp.exp(a_log_row), 0.0)


def _load_rows(ref):
    if len(ref.shape) == 2:
        return ref[...]
    return jnp.concatenate([ref[:, t, :] for t in range(ref.shape[1])], axis=0)


def _store_rows(ref, val):
    if len(ref.shape) == 2:
        ref[...] = val.astype(ref.dtype)
    else:
        nb = ref.shape[0]
        for t in range(ref.shape[1]):
            ref[:, t, :] = val[t * nb:(t + 1) * nb, :].astype(ref.dtype)


def _dot(a, b):
    return jnp.dot(a.astype(BF16), b.astype(BF16), preferred_element_type=F32)


def _split3(x):
    hi = x.astype(BF16)
    r1 = x - hi.astype(F32)
    mid = r1.astype(BF16)
    lo = (r1 - mid.astype(F32)).astype(BF16)
    return hi, mid, lo


def _pack_heads(v):
    lane = lax.broadcasted_iota(jnp.int32, v.shape, 1)
    hi, mid, lo = _split3(jnp.where(lane < SSM_HEADS, v, 0.0))
    packed = (hi.astype(F32) + pltpu.roll(mid.astype(F32), HEAD_SLOT, axis=1)
              + pltpu.roll(lo.astype(F32), 2 * HEAD_SLOT, axis=1))
    return packed.astype(BF16)


def _expand_heads(v, expand3):
    return jnp.dot(_pack_heads(v), expand3, preferred_element_type=F32)


def _chunk_cumsum(tri16, a):
    s = jnp.dot(tri16, _pack_heads(a), preferred_element_type=F32)
    return s + pltpu.roll(s, DT_PAD - HEAD_SLOT, axis=1) + pltpu.roll(s, DT_PAD - 2 * HEAD_SLOT, axis=1)


def _in_proj_kernel(x_ref, g_ref, w_ref, u_ref, z_ref, xbc_ref, dt_ref):
    xn = _rms(_load_rows(x_ref), g_ref[...]).astype(BF16)
    _store_rows(u_ref, jnp.dot(xn, w_ref[:, 0:D_POOL], preferred_element_type=F32))
    _store_rows(z_ref, jnp.dot(xn, w_ref[:, D_POOL:D_POOL + D_SSM], preferred_element_type=F32))
    _store_rows(xbc_ref, jnp.dot(xn, w_ref[:, D_POOL + D_SSM:D_POOL + D_SSM + CONV_DIM],
                                 preferred_element_type=F32))
    _store_rows(dt_ref, jnp.dot(xn, w_ref[:, D_POOL + D_SSM + CONV_DIM:D_IN_PAD],
                                preferred_element_type=F32))


def _row_spec(a_shape, tm, width):
    if len(a_shape) == 2:
        return pl.BlockSpec((tm, width), lambda i: (i, 0))
    return pl.BlockSpec((tm, a_shape[1], width), lambda i: (i, 0, 0))


def _in_proj(x, g, w_in_pad, tm):
    lead = x.shape[:-1]
    fixed = lambda i: (0, 0)
    widths = (D_POOL, D_SSM, CONV_DIM, DT_PAD)
    return pl.pallas_call(
        _in_proj_kernel,
        grid=(lead[0] // tm,),
        in_specs=[
            _row_spec(x.shape, tm, D_MODEL),
            pl.BlockSpec((1, D_MODEL), fixed),
            pl.BlockSpec((D_MODEL, D_IN_PAD), fixed),
        ],
        out_specs=[_row_spec(x.shape, tm, w) for w in widths],
        out_shape=[jax.ShapeDtypeStruct(lead + (w,), F32) for w in widths],
        compiler_params=pltpu.CompilerParams(
            dimension_semantics=("arbitrary",), vmem_limit_bytes=VMEM_LIMIT),
        name="in_proj",
    )(x, g, w_in_pad)


def _run(steps):
    try:
        while True:
            next(steps)
    except StopIteration as done:
        return done.value


def _interleave(main, main_total, side, side_total):
    result = None
    main_done = side_done = False
    main_cost = side_cost = 0
    while not (main_done and side_done):
        if side_done or (not main_done and main_cost * side_total <= side_cost * main_total):
            try:
                main_cost += next(main)
            except StopIteration as done:
                result, main_done = done.value, True
        else:
            try:
                side_cost += next(side)
            except StopIteration:
                side_done = True
    return result


def _ffn_tile_steps(x, mix16, p, wout_ref, g_mlp_ref, w1_ref, w2_ref, g_ple_ref, wg_ref, wple_ref, g_fin_ref):
    h = x + jnp.dot(mix16, wout_ref[...], preferred_element_type=F32)
    yield FFN_COST_OUT_PROJ
    hn = _rms(h, g_mlp_ref[...]).astype(BF16)
    ff_block = D_FF // FFN_STAGES
    for c in range(FFN_STAGES):
        f = jnp.dot(hn, w1_ref[:, c * ff_block:(c + 1) * ff_block], preferred_element_type=F32)
        f = jnp.square(jnp.maximum(f, 0.0)).astype(BF16)
        yield FFN_COST_FF_HALF
        h = h + jnp.dot(f, w2_ref[c * ff_block:(c + 1) * ff_block, :], preferred_element_type=F32)
        yield FFN_COST_FF_HALF
    gate = jax.nn.sigmoid(
        jnp.dot(_rms(h, g_ple_ref[...]).astype(BF16), wg_ref[...], preferred_element_type=F32))
    h = h + gate * jnp.dot(p.astype(BF16), wple_ref[...], preferred_element_type=F32)
    return _rms(h, g_fin_ref[...])


def _out_ffn_kernel(x_ref, mix_ref, p_ref, *refs):
    *weight_refs, y_ref = refs
    _store_rows(y_ref, _run(_ffn_tile_steps(_load_rows(x_ref), _load_rows(mix_ref).astype(BF16),
                                            _load_rows(p_ref), *weight_refs)))


def _out_ffn(x, mix, p, wout, g_mlp, w1, w2, g_ple, wg, wple, g_fin, tm):
    fixed = lambda i: (0, 0)

    def resident(shape):
        return pl.BlockSpec(shape, fixed, pipeline_mode=pl.Buffered(1))

    return pl.pallas_call(
        _out_ffn_kernel,
        grid=(x.shape[0] // tm,),
        in_specs=[
            _row_spec(x.shape, tm, D_MODEL),
            _row_spec(x.shape, tm, 2 * D_MODEL),
            _row_spec(x.shape, tm, D_PLE),
            resident((2 * D_MODEL, D_MODEL)),
            resident((1, D_MODEL)),
            resident((D_MODEL, D_FF)),
            resident((D_FF, D_MODEL)),
            resident((1, D_MODEL)),
            resident((D_MODEL, D_MODEL)),
            resident((D_PLE, D_MODEL)),
            resident((1, D_MODEL)),
        ],
        out_specs=_row_spec(x.shape, tm, D_MODEL),
        out_shape=jax.ShapeDtypeStruct(x.shape, F32),
        compiler_params=pltpu.CompilerParams(
            dimension_semantics=("arbitrary",), vmem_limit_bytes=VMEM_LIMIT),
        name="out_ffn",
    )(x, mix, p, wout, g_mlp, w1, w2, g_ple, wg, wple, g_fin)


PROMPT_TILE = 256


def _mix_chunk_steps(u, z, xbc, dt_raw, pos0, keep, const_refs, pool_hist, conv_hist, state_t):
    (wpool_ref, pscale_ref, convw_ref, convb_ref, dtb_ref, alog_ref, dskip_ref, gssm_ref, tri_ref,
     expand_ref) = const_refs
    carried = (lambda v: v) if keep is None else (lambda v: v * keep)

    ext = jnp.concatenate([carried(pool_hist[...]), u], axis=0)
    pool_hist[...] = u[CHUNK - POOL_HIST:, :]
    pos = pos0 + lax.broadcasted_iota(jnp.int32, (CHUNK, 1), 0)
    y_pool = []
    for g, w in enumerate(POOL_WINDOWS):
        cols = slice(g * POOL_GROUP, (g + 1) * POOL_GROUP)
        s = ext[:, cols]
        shift = 1
        while shift < w:
            s = s + pltpu.roll(s, shift, axis=0)
            shift *= 2
        count = jnp.minimum(w, pos + 1).astype(F32)
        d = s[POOL_HIST:, :] / count - u[:, cols]
        y_pool.append(_dot(d, wpool_ref[g]))
    y_pool = jnp.concatenate(y_pool, axis=1) * pscale_ref[...]
    yield MIX_COST_POOL

    extc = jnp.concatenate([carried(conv_hist[...]), xbc], axis=0)
    conv_hist[...] = xbc[CHUNK - CONV_HIST:, :]
    acc = convb_ref[...]
    for k in range(CONV_WIDTH):
        back = CONV_WIDTH - 1 - k
        tap = extc if back == 0 else pltpu.roll(extc, back, axis=0)
        acc = acc + tap[CONV_HIST:, :] * convw_ref[k:k + 1, :]
    xc = _silu(acc)
    xs = xc[:, 0:D_SSM]
    b_all = xc[:, D_SSM:D_SSM + SSM_GROUPS * D_STATE]
    c_all = xc[:, D_SSM + SSM_GROUPS * D_STATE:CONV_DIM]
    yield MIX_COST_CONV

    dt = _softplus(dt_raw + dtb_ref[...])
    a_cum = _chunk_cumsum(tri_ref[...], dt * _neg_exp_heads(alog_ref[...]))
    a_cum_t = a_cum.T
    a_last = a_cum[CHUNK - 1:CHUNK, :]
    expand = expand_ref[...]
    dt_x = _expand_heads(dt, expand)
    ea_x = _expand_heads(jnp.exp(a_cum), expand)
    de_x = _expand_heads(jnp.exp(a_last - a_cum), expand)
    cd_x = _expand_heads(jnp.broadcast_to(jnp.exp(a_last), (2 * SUBLANES, DT_PAD)), expand)[0:1, :]
    x_dt = xs * dt_x
    x_w = x_dt * de_x
    yield MIX_COST_PREP

    row_i = lax.broadcasted_iota(jnp.int32, (CHUNK, CHUNK), 0)
    col_i = lax.broadcasted_iota(jnp.int32, (CHUNK, CHUNK), 1)
    causal = row_i >= col_i
    lane = lax.broadcasted_iota(jnp.int32, (CHUNK, 2 * HEAD_DIM), 1)
    first_head = lane < HEAD_DIM

    y_parts = []
    for g in range(SSM_GROUPS):
        gch = slice(g * GROUP_WIDTH, (g + 1) * GROUP_WIDTH)
        b_g = b_all[:, g * D_STATE:(g + 1) * D_STATE].astype(BF16)
        c_g = c_all[:, g * D_STATE:(g + 1) * D_STATE].astype(BF16)
        cb = lax.dot_general(c_g, b_g, (((1,), (1,)), ((), ())), preferred_element_type=F32)
        st_g = carried(state_t[:, gch])
        y_off = jnp.dot(c_g, st_g.astype(BF16), preferred_element_type=F32) * ea_x[:, gch]
        y_diag = []
        for j in range(HEADS_PER_GROUP // 2):
            h0 = g * HEADS_PER_GROUP + 2 * j
            m = []
            for h in (h0, h0 + 1):
                seg = a_cum[:, h:h + 1] - a_cum_t[h:h + 1, :]
                m.append((cb * jnp.exp(jnp.where(causal, seg, -jnp.inf))).astype(BF16))
            pair = x_dt[:, h0 * HEAD_DIM:(h0 + 2) * HEAD_DIM]
            rhs = jnp.concatenate([jnp.where(first_head, pair, 0.0),
                                   jnp.where(first_head, 0.0, pair)], axis=0).astype(BF16)
            y_diag.append(jnp.dot(jnp.concatenate(m, axis=1), rhs, preferred_element_type=F32))
            yield MIX_COST_HEAD_PAIR
        y_parts.append(jnp.concatenate(y_diag, axis=1) + y_off)
        upd = lax.dot_general(b_g, x_w[:, gch].astype(BF16), (((0,), (0,)), ((), ())),
                              preferred_element_type=F32)
        state_t[:, gch] = st_g * cd_x[:, gch] + upd

    y = jnp.concatenate(y_parts, axis=1) + dskip_ref[...] * xs
    y = y * _silu(z)
    return y_pool, _rms(y, gssm_ref[...])


N_MIX_CONSTS = 10
N_FFN_WEIGHTS = 8


def _prompt_mix_ffn_kernel(u_ref, z_ref, xbc_ref, dt_ref, x_ref, p_ref, *refs, n_tiles, tiles_per_seq):
    const_refs = refs[:N_MIX_CONSTS]
    weight_refs = refs[N_MIX_CONSTS:N_MIX_CONSTS + N_FFN_WEIGHTS]
    y_ref, ssm_ref, mix_scr, pool_hist, conv_hist, state_t = refs[N_MIX_CONSTS + N_FFN_WEIGHTS:]
    s = pl.program_id(0)

    @pl.when(s == 0)
    def _():
        mix_scr[...] = jnp.zeros_like(mix_scr)
        pool_hist[...] = jnp.zeros_like(pool_hist)
        conv_hist[...] = jnp.zeros_like(conv_hist)
        state_t[...] = jnp.zeros_like(state_t)

    tile_in_seq = jnp.minimum(s, n_tiles - 1) % tiles_per_seq
    keep = jnp.where(tile_in_seq == 0, 0.0, 1.0)
    slot = s % 2

    def mixer_steps():
        for ci in range(PROMPT_TILE // CHUNK):
            rows = slice(ci * CHUNK, (ci + 1) * CHUNK)
            y_pool, y_ssm = yield from _mix_chunk_steps(
                u_ref[rows, :], z_ref[rows, :], xbc_ref[rows, :], dt_ref[rows, :],
                tile_in_seq * PROMPT_TILE + ci * CHUNK, keep if ci == 0 else None,
                const_refs, pool_hist, conv_hist, state_t)
            mix_scr[slot, rows, 0:D_POOL] = y_pool.astype(BF16)
            mix_scr[slot, rows, D_POOL:] = y_ssm.astype(BF16)
            yield MIX_COST_TAIL

    ffn_steps = _ffn_tile_steps(x_ref[...], mix_scr[(s + 1) % 2], p_ref[...], *weight_refs)
    mix_total = (PROMPT_TILE // CHUNK) * (MIX_COST_POOL + MIX_COST_CONV + MIX_COST_PREP
                                          + SSM_HEADS // 2 * MIX_COST_HEAD_PAIR + MIX_COST_TAIL)
    y_ref[...] = _interleave(ffn_steps, FFN_COST_TOTAL, mixer_steps(), mix_total)

    @pl.when((tile_in_seq == tiles_per_seq - 1) & (s < n_tiles))
    def _():
        for j in range(D_SSM // LANES):
            ssm_ref[j * LANES:(j + 1) * LANES, :] = state_t[:, j * LANES:(j + 1) * LANES].T


def _prompt_mix_ffn(u, z, xbc, dt, x2d, p2d, consts, ffn_weights, batch, seq):
    tiles_per_seq = seq // PROMPT_TILE
    n_tiles = batch * tiles_per_seq
    mix_row = lambda s: (jnp.minimum(s, n_tiles - 1), 0)
    ffn_row = lambda s: (jnp.maximum(s - 1, 0), 0)
    fixed2 = lambda s: (0, 0)
    fixed3 = lambda s: (0, 0, 0)

    def resident(a):
        return pl.BlockSpec(a.shape, fixed3 if a.ndim == 3 else fixed2, pipeline_mode=pl.Buffered(1))

    return pl.pallas_call(
        functools.partial(_prompt_mix_ffn_kernel, n_tiles=n_tiles, tiles_per_seq=tiles_per_seq),
        grid=(n_tiles + 1,),
        in_specs=[
            pl.BlockSpec((PROMPT_TILE, D_POOL), mix_row),
            pl.BlockSpec((PROMPT_TILE, D_SSM), mix_row),
            pl.BlockSpec((PROMPT_TILE, CONV_DIM), mix_row),
            pl.BlockSpec((PROMPT_TILE, DT_PAD), mix_row),
            pl.BlockSpec((PROMPT_TILE, D_MODEL), ffn_row),
            pl.BlockSpec((PROMPT_TILE, D_PLE), ffn_row),
        ] + [resident(a) for a in consts] + [resident(a) for a in ffn_weights],
        out_specs=[
            pl.BlockSpec((PROMPT_TILE, D_MODEL), ffn_row),
            pl.BlockSpec((None, D_SSM, D_STATE),
                         lambda s: (jnp.minimum(s, n_tiles - 1) // tiles_per_seq, 0, 0)),
        ],
        out_shape=[
            jax.ShapeDtypeStruct((batch * seq, D_MODEL), F32),
            jax.ShapeDtypeStruct((batch, D_SSM, D_STATE), F32),
        ],
        scratch_shapes=[
            pltpu.VMEM((2, PROMPT_TILE, 2 * D_MODEL), BF16),
            pltpu.VMEM((POOL_HIST, D_POOL), F32),
            pltpu.VMEM((CONV_HIST, CONV_DIM), F32),
            pltpu.VMEM((D_STATE, D_SSM), F32),
        ],
        compiler_params=pltpu.CompilerParams(
            dimension_semantics=("arbitrary",), vmem_limit_bytes=VMEM_LIMIT),
        name="prompt_mix_ffn",
    )(u, z, xbc, dt, x2d, p2d, *consts, *ffn_weights)


SAMPLE_T = 4
SAMPLE_BB = 8


def _sample_mix_kernel(u_ref, z_ref, xbc_ref, dt_ref, pbuf_ref, cbuf_ref, state_ref,
                       wpool_ref, pscale_ref, convw_ref, convb_ref, dtb_ref, alog_ref, dskip_ref,
                       gssm_ref, expand_ref,
                       mix_ref, npool_ref, nconv_ref, nstate_ref):
    nt, bb = SAMPLE_T, SAMPLE_BB
    rows = nt * bb

    at_time = lambda ref, t: ref[:, t, :]
    ext = [pbuf_ref[:, i, :] for i in range(POOL_BUF)] + [at_time(u_ref, t) for t in range(nt)]
    for i in range(POOL_BUF):
        npool_ref[:, i, :] = ext[i + nt]
    y_pool = []
    for g, w in enumerate(POOL_WINDOWS):
        cols = slice(g * POOL_GROUP, (g + 1) * POOL_GROUP)
        d = []
        for t in range(nt):
            hi = POOL_BUF + t
            win = ext[hi - w + 1][:, cols]
            for i in range(hi - w + 2, hi + 1):
                win = win + ext[i][:, cols]
            count = float(min(w, PAST_LEN + t + 1))
            d.append(win / count - ext[hi][:, cols])
        y_pool.append(_dot(jnp.concatenate(d, axis=0), wpool_ref[g]))
    y_pool = jnp.concatenate(y_pool, axis=1) * pscale_ref[...]

    extc = [cbuf_ref[:, i, :] for i in range(CONV_WIDTH - 1)] + [at_time(xbc_ref, t) for t in range(nt)]
    for j in range(CONV_WIDTH - 1):
        nconv_ref[:, j, :] = extc[nt + j]
    xc = []
    for t in range(nt):
        acc = convb_ref[...]
        for k in range(CONV_WIDTH):
            acc = acc + extc[t + k] * convw_ref[k:k + 1, :]
        xc.append(_silu(acc))
    xs = [v[:, 0:D_SSM] for v in xc]
    bm = [v[:, D_SSM:D_SSM + SSM_GROUPS * D_STATE] for v in xc]
    cm = [v[:, D_SSM + SSM_GROUPS * D_STATE:CONV_DIM] for v in xc]

    a_row = _neg_exp_heads(alog_ref[...])
    dt = [_softplus(at_time(dt_ref, t) + dtb_ref[...]) for t in range(nt)]
    a_cum = []
    for t in range(nt):
        a_t = dt[t] * a_row
        a_cum.append(a_t if t == 0 else a_cum[-1] + a_t)
    lane = lax.broadcasted_iota(jnp.int32, (bb, DT_PAD), 1)
    in_group0 = lane < HEADS_PER_GROUP
    head_rows = []
    pair_index = {}
    for l in range(nt):
        for s in range(l + 1):
            cb = [jnp.sum(cm[l][:, g * D_STATE:(g + 1) * D_STATE] * bm[s][:, g * D_STATE:(g + 1) * D_STATE],
                          axis=-1, keepdims=True) for g in range(SSM_GROUPS)]
            cb_heads = jnp.where(in_group0, cb[0], cb[1])
            pair_index[(l, s)] = len(head_rows)
            head_rows.append(cb_heads * jnp.exp(a_cum[l] - a_cum[s]) * dt[s])
    ea_index = len(head_rows)
    head_rows += [jnp.exp(a_cum[t]) for t in range(nt)]
    w_index = len(head_rows)
    head_rows += [jnp.exp(a_cum[nt - 1] - a_cum[t]) * dt[t] for t in range(nt)]
    cd_index = len(head_rows)
    head_rows.append(jnp.exp(a_cum[nt - 1]))
    chan = _expand_heads(jnp.concatenate(head_rows, axis=0), expand_ref[...])
    chan_rows = lambda i: chan[i * bb:(i + 1) * bb, :]

    y_diag = []
    for l in range(nt):
        acc = chan_rows(pair_index[(l, 0)]) * xs[0]
        for s in range(1, l + 1):
            acc = acc + chan_rows(pair_index[(l, s)]) * xs[s]
        y_diag.append(acc)
    xs_all = jnp.concatenate(xs, axis=0)
    xw_all = jnp.concatenate([chan_rows(w_index + t) * xs[t] for t in range(nt)], axis=0)
    ea_all = jnp.concatenate([chan_rows(ea_index + t) for t in range(nt)], axis=0)
    b_all = jnp.concatenate(bm, axis=0).astype(BF16)
    c_all = jnp.concatenate(cm, axis=0).astype(BF16)
    cd3 = jnp.concatenate([p.astype(F32) for p in _split3(chan_rows(cd_index))], axis=0)
    ones3 = jnp.ones((3 * bb, D_STATE), BF16)
    row_batch = lax.broadcasted_iota(jnp.int32, (rows, 1), 0) & (bb - 1)
    row_batch3 = lax.broadcasted_iota(jnp.int32, (3 * bb, 1), 0) & (bb - 1)
    contract_rows = (((0,), (0,)), ((), ()))

    def per_batch(b, y_off):
        st = state_ref[b]
        st16 = st.astype(BF16)
        mine = row_batch == b
        xw_b = jnp.where(mine, xw_all, 0.0).astype(BF16)
        cd_b = jnp.where(row_batch3 == b, cd3, 0.0).astype(BF16)
        scale = lax.dot_general(cd_b, ones3, contract_rows, preferred_element_type=F32)
        yo = []
        for g in range(SSM_GROUPS):
            grow = slice(g * GROUP_WIDTH, (g + 1) * GROUP_WIDTH)
            gst = slice(g * D_STATE, (g + 1) * D_STATE)
            yo.append(lax.dot_general(c_all[:, gst], st16[grow, :], (((1,), (1,)), ((), ())),
                                      preferred_element_type=F32))
            upd = lax.dot_general(xw_b[:, grow], b_all[:, gst], contract_rows,
                                  preferred_element_type=F32)
            nstate_ref[b, grow, :] = st[grow, :] * scale[grow, :] + upd
        return jnp.where(mine, jnp.concatenate(yo, axis=1), y_off)

    y_off = lax.fori_loop(0, bb, per_batch, jnp.zeros((rows, D_SSM), F32), unroll=True)
    y = jnp.concatenate(y_diag, axis=0) + y_off * ea_all + dskip_ref[...] * xs_all
    y = y * _silu(jnp.concatenate([at_time(z_ref, t) for t in range(nt)], axis=0))
    y_ssm = _rms(y, gssm_ref[...])
    for t in range(nt):
        mix_ref[:, t, 0:D_POOL] = y_pool[t * bb:(t + 1) * bb, :].astype(mix_ref.dtype)
        mix_ref[:, t, D_POOL:] = y_ssm[t * bb:(t + 1) * bb, :].astype(mix_ref.dtype)


def _sample_group_mix(u, z, xbc, dt, pbuf, cbuf, state, consts):
    nt, bb = SAMPLE_T, SAMPLE_BB
    batch = state.shape[0]
    (wpool, pscale, convw, convb, dtb, alog, dskip_x, gssm, _, expand) = consts
    blk3 = lambda i: (i, 0, 0)
    fixed2 = lambda i: (0, 0)
    fixed3 = lambda i: (0, 0, 0)
    return pl.pallas_call(
        _sample_mix_kernel,
        grid=(batch // bb,),
        in_specs=[
            pl.BlockSpec((bb, nt, D_POOL), blk3),
            pl.BlockSpec((bb, nt, D_SSM), blk3),
            pl.BlockSpec((bb, nt, CONV_DIM), blk3),
            pl.BlockSpec((bb, nt, DT_PAD), blk3),
            pl.BlockSpec((bb, POOL_BUF, D_POOL), blk3),
            pl.BlockSpec((bb, CONV_WIDTH - 1, CONV_DIM), blk3),
            pl.BlockSpec((bb, D_SSM, D_STATE), blk3),
            pl.BlockSpec((len(POOL_WINDOWS), POOL_GROUP, POOL_GROUP), fixed3),
            pl.BlockSpec((1, D_POOL), fixed2),
            pl.BlockSpec((CONV_WIDTH, CONV_DIM), fixed2),
            pl.BlockSpec((1, CONV_DIM), fixed2),
            pl.BlockSpec((1, DT_PAD), fixed2),
            pl.BlockSpec((1, DT_PAD), fixed2),
            pl.BlockSpec((1, D_SSM), fixed2),
            pl.BlockSpec((1, D_SSM), fixed2),
            pl.BlockSpec((DT_PAD, D_SSM), fixed2),
        ],
        out_specs=[
            pl.BlockSpec((bb, nt, 2 * D_MODEL), blk3),
            pl.BlockSpec((bb, POOL_BUF, D_POOL), blk3),
            pl.BlockSpec((bb, CONV_WIDTH - 1, CONV_DIM), blk3),
            pl.BlockSpec((bb, D_SSM, D_STATE), blk3),
        ],
        out_shape=[
            jax.ShapeDtypeStruct((batch, nt, 2 * D_MODEL), F32),
            jax.ShapeDtypeStruct((batch, POOL_BUF, D_POOL), F32),
            jax.ShapeDtypeStruct((batch, CONV_WIDTH - 1, CONV_DIM), F32),
            jax.ShapeDtypeStruct((batch, D_SSM, D_STATE), F32),
        ],
        compiler_params=pltpu.CompilerParams(
            dimension_semantics=("arbitrary",), vmem_limit_bytes=VMEM_LIMIT),
        name="sample_mix",
    )(u, z, xbc, dt, pbuf, cbuf, state, wpool, pscale, convw, convb, dtb, alog, dskip_x, gssm, expand)


def _mix_constants(w_pool, pool_scale, conv_w, conv_b, dt_bias, a_log, d_skip, ssm_norm_g):
    pad = DT_PAD - SSM_HEADS
    head_of_channel = jnp.arange(D_SSM, dtype=jnp.int32) // HEAD_DIM
    slot_row = jnp.arange(DT_PAD, dtype=jnp.int32)
    expand = ((slot_row[:, None] % HEAD_SLOT == head_of_channel[None, :])
              & (slot_row[:, None] < 3 * HEAD_SLOT)).astype(BF16)
    tri = (jnp.arange(CHUNK)[:, None] >= jnp.arange(CHUNK)[None, :]).astype(BF16)
    return (
        w_pool.astype(BF16),
        pool_scale.reshape(1, D_POOL),
        conv_w,
        conv_b.reshape(1, CONV_DIM),
        jnp.pad(dt_bias, (0, pad)).reshape(1, DT_PAD),
        jnp.pad(a_log, (0, pad)).reshape(1, DT_PAD),
        jnp.repeat(d_skip, HEAD_DIM).reshape(1, D_SSM),
        ssm_norm_g.reshape(1, D_SSM),
        tri,
        expand,
    )


PROMPT_IN_TM = 512
SAMPLE_TB = 64


def kernel(x_prompt, x_sample, p_prompt, p_sample, state_pool, state_conv, state_ssm, w_in, w_pool, pool_scale, conv_w, conv_b, dt_bias, a_log, d_skip, ssm_norm_g, w_out, norm_mix_g, norm_mlp_g, w_ff1, w_ff2, norm_ple_g, w_gate, w_ple, final_norm_g):
    assert w_in.shape[0] == 1, "single-layer trunk"
    bp, tp = x_prompt.shape[:2]
    bs, ts = x_sample.shape[:2]
    assert tp % PROMPT_TILE == 0 and tp >= POOL_BUF and ts == SAMPLE_T and bs % SAMPLE_BB == 0

    row = lambda v: v.reshape(1, -1)
    w_in_pad = jnp.pad(w_in[0], ((0, 0), (0, D_IN_PAD - w_in.shape[2]))).astype(BF16)
    consts = _mix_constants(w_pool[0], pool_scale[0], conv_w[0], conv_b[0], dt_bias[0], a_log[0],
                            d_skip[0], ssm_norm_g[0])
    ffn_weights = (w_out[0].astype(BF16), row(norm_mlp_g[0]), w_ff1[0].astype(BF16),
                   w_ff2[0].astype(BF16), row(norm_ple_g[0]), w_gate[0].astype(BF16),
                   w_ple[0].astype(BF16), row(final_norm_g))
    g_mix = row(norm_mix_g[0])

    xp = x_prompt.reshape(bp * tp, D_MODEL)
    u, z, xbc, dt = _in_proj(xp, g_mix, w_in_pad, PROMPT_IN_TM)
    y_prompt, ssm_p = _prompt_mix_ffn(u, z, xbc, dt, xp, p_prompt[0].reshape(bp * tp, D_PLE), consts,
                                      ffn_weights, bp, tp)
    pool_p = u.reshape(bp, tp, D_POOL)[:, tp - POOL_BUF:, :]
    conv_p = xbc.reshape(bp, tp, CONV_DIM)[:, tp - (CONV_WIDTH - 1):, :]

    u_s, z_s, xbc_s, dt_s = _in_proj(x_sample, g_mix, w_in_pad, SAMPLE_TB)
    mix_s, pool_s, conv_s, ssm_s = _sample_group_mix(
        u_s, z_s, xbc_s, dt_s, state_pool[0], state_conv[0], state_ssm[0].reshape(bs, D_SSM, D_STATE), consts)
    y_sample = _out_ffn(x_sample, mix_s, p_sample[0], *ffn_weights, SAMPLE_TB)

    state_shape = (1, -1, SSM_HEADS, HEAD_DIM, D_STATE)
    return (y_prompt.reshape(bp, tp, D_MODEL), y_sample,
            pool_p[None], conv_p[None], ssm_p.reshape(state_shape),
            pool_s[None], conv_s[None], ssm_s.reshape(state_shape))
```

```python
import functools
import math

import jax
import jax.numpy as jnp
from jax import lax
from jax.experimental import pallas as pl
from jax.experimental.pallas import tpu as pltpu

F32 = jnp.float32
BF16 = jnp.bfloat16

D_MODEL = 1024
D_POOL = 1024
D_SSM = 1024
POOL_WINDOWS = (2, 4, 8, 16)
POOL_GROUP = 256
POOL_BUF = 15
SSM_HEADS = 16
HEAD_DIM = 64
SSM_GROUPS = 2
HEADS_PER_GROUP = 8
GROUP_WIDTH = HEADS_PER_GROUP * HEAD_DIM
D_STATE = 128
CONV_WIDTH = 4
CONV_DIM = D_SSM + 2 * SSM_GROUPS * D_STATE
CHUNK = 128
D_FF = 4096
D_PLE = 256
PAST_LEN = 16384
EPS = 1e-6

LANES = 128
SUBLANES = 8
DT_PAD = LANES
HEAD_SLOT = SSM_HEADS
D_IN_PAD = D_POOL + D_SSM + CONV_DIM + DT_PAD
POOL_HIST = 16
CONV_HIST = SUBLANES
VMEM_LIMIT = 56 * 1024 * 1024
FFN_STAGES = 4
FFN_COST_OUT_PROJ = 2048
FFN_COST_FF_HALF = 1024
FFN_COST_TOTAL = FFN_COST_OUT_PROJ + 2 * FFN_STAGES * FFN_COST_FF_HALF + 1280
MIX_COST_POOL = 350
MIX_COST_CONV = 900
MIX_COST_PREP = 450
MIX_COST_HEAD_PAIR = 230
MIX_COST_TAIL = 350


def _rms(x, g):
    return x * lax.rsqrt(jnp.mean(x * x, axis=-1, keepdims=True) + EPS) * g


def _silu(x):
    return x * jax.nn.sigmoid(x)


def _softplus(x):
    return jnp.maximum(x, 0.0) + jnp.log1p(jnp.exp(-jnp.abs(x)))


def _neg_exp_heads(a_log_row):
    lane = lax.broadcasted_iota(jnp.int32, a_log_row.shape, a_log_row.ndim - 1)
    return jnp.where(lane < SSM_HEADS, -jnp.exp(a_log_row), 0.0)


def _load_rows(ref):
    if len(ref.shape) == 2:
        return ref[...]
    return jnp.concatenate([ref[:, t, :] for t in range(ref.shape[1])], axis=0)


def _store_rows(ref, val):
    if len(ref.shape) == 2:
        ref[...] = val.astype(ref.dtype)
    else:
        nb = ref.shape[0]
        for t in range(ref.shape[1]):
            ref[:, t, :] = val[t * nb:(t + 1) * nb, :].astype(ref.dtype)


def _dot(a, b):
    return jnp.dot(a.astype(BF16), b.astype(BF16), preferred_element_type=F32)


def _split3(x):
    hi = x.astype(BF16)
    r1 = x - hi.astype(F32)
    mid = r1.astype(BF16)
    lo = (r1 - mid.astype(F32)).astype(BF16)
    return hi, mid, lo


def _pack_heads(v):
    lane = lax.broadcasted_iota(jnp.int32, v.shape, 1)
    hi, mid, lo = _split3(jnp.where(lane < SSM_HEADS, v, 0.0))
    packed = (hi.astype(F32) + pltpu.roll(mid.astype(F32), HEAD_SLOT, axis=1)
              + pltpu.roll(lo.astype(F32), 2 * HEAD_SLOT, axis=1))
    return packed.astype(BF16)


def _expand_heads(v, expand3):
    return jnp.dot(_pack_heads(v), expand3, preferred_element_type=F32)


def _chunk_cumsum(tri16, a):
    s = jnp.dot(tri16, _pack_heads(a), preferred_element_type=F32)
    return s + pltpu.roll(s, DT_PAD - HEAD_SLOT, axis=1) + pltpu.roll(s, DT_PAD - 2 * HEAD_SLOT, axis=1)


def _in_proj_kernel(x_ref, g_ref, wt_ref, u_ref, z_ref, xbc_ref, dt_ref):
    xn = _rms(_load_rows(x_ref), g_ref[...]).astype(BF16)
    start = 0
    for out_ref, width in ((u_ref, D_POOL), (z_ref, D_SSM), (xbc_ref, CONV_DIM), (dt_ref, DT_PAD)):
        _store_rows(out_ref, lax.dot_general(xn, wt_ref[start:start + width, :], (((1,), (1,)), ((), ())),
                                             preferred_element_type=F32))
        start += width


def _row_spec(a_shape, tm, width):
    if len(a_shape) == 2:
        return pl.BlockSpec((tm, width), lambda i: (i, 0))
    return pl.BlockSpec((tm, a_shape[1], width), lambda i: (i, 0, 0))


def _in_proj(x, g, w_in_t, tm):
    lead = x.shape[:-1]
    fixed = lambda i: (0, 0)
    widths = (D_POOL, D_SSM, CONV_DIM, DT_PAD)
    return pl.pallas_call(
        _in_proj_kernel,
        grid=(lead[0] // tm,),
        in_specs=[
            _row_spec(x.shape, tm, D_MODEL),
            pl.BlockSpec((1, D_MODEL), fixed),
            pl.BlockSpec((D_IN_PAD, D_MODEL), fixed),
        ],
        out_specs=[_row_spec(x.shape, tm, w) for w in widths],
        out_shape=[jax.ShapeDtypeStruct(lead + (w,), F32) for w in widths],
        compiler_params=pltpu.CompilerParams(
            dimension_semantics=("arbitrary",), vmem_limit_bytes=VMEM_LIMIT),
        name="in_proj",
    )(x, g, w_in_t)


def _run(steps):
    try:
        while True:
            next(steps)
    except StopIteration as done:
        return done.value


def _interleave(main, main_total, side, side_total):
    result = None
    main_done = side_done = False
    main_cost = side_cost = 0
    while not (main_done and side_done):
        if side_done or (not main_done and main_cost * side_total <= side_cost * main_total):
            try:
                main_cost += next(main)
            except StopIteration as done:
                result, main_done = done.value, True
        else:
            try:
                side_cost += next(side)
            except StopIteration:
                side_done = True
    return result


def _ffn_tile_steps(x, mix16, p, wout_ref, g_mlp_ref, w1_ref, w2_ref, g_ple_ref, wg_ref, wple_ref, g_fin_ref):
    h = x + jnp.dot(mix16, wout_ref[...], preferred_element_type=F32)
    yield FFN_COST_OUT_PROJ
    hn = _rms(h, g_mlp_ref[...]).astype(BF16)
    ff_block = D_FF // FFN_STAGES
    for c in range(FFN_STAGES):
        f = jnp.dot(hn, w1_ref[:, c * ff_block:(c + 1) * ff_block], preferred_element_type=F32)
        f = jnp.square(jnp.maximum(f, 0.0)).astype(BF16)
        yield FFN_COST_FF_HALF
        h = h + jnp.dot(f, w2_ref[c * ff_block:(c + 1) * ff_block, :], preferred_element_type=F32)
        yield FFN_COST_FF_HALF
    gate = jax.nn.sigmoid(
        jnp.dot(_rms(h, g_ple_ref[...]).astype(BF16), wg_ref[...], preferred_element_type=F32))
    h = h + gate * jnp.dot(p.astype(BF16), wple_ref[...], preferred_element_type=F32)
    return _rms(h, g_fin_ref[...])


def _out_ffn_kernel(x_ref, mix_ref, p_ref, *refs):
    *weight_refs, y_ref = refs
    _store_rows(y_ref, _run(_ffn_tile_steps(_load_rows(x_ref), _load_rows(mix_ref).astype(BF16),
                                            _load_rows(p_ref), *weight_refs)))


def _out_ffn(x, mix, p, wout, g_mlp, w1, w2, g_ple, wg, wple, g_fin, tm):
    fixed = lambda i: (0, 0)

    def resident(shape):
        return pl.BlockSpec(shape, fixed, pipeline_mode=pl.Buffered(1))

    return pl.pallas_call(
        _out_ffn_kernel,
        grid=(x.shape[0] // tm,),
        in_specs=[
            _row_spec(x.shape, tm, D_MODEL),
            _row_spec(x.shape, tm, 2 * D_MODEL),
            _row_spec(x.shape, tm, D_PLE),
            resident((2 * D_MODEL, D_MODEL)),
            resident((1, D_MODEL)),
            resident((D_MODEL, D_FF)),
            resident((D_FF, D_MODEL)),
            resident((1, D_MODEL)),
            resident((D_MODEL, D_MODEL)),
            resident((D_PLE, D_MODEL)),
            resident((1, D_MODEL)),
        ],
        out_specs=_row_spec(x.shape, tm, D_MODEL),
        out_shape=jax.ShapeDtypeStruct(x.shape, F32),
        compiler_params=pltpu.CompilerParams(
            dimension_semantics=("arbitrary",), vmem_limit_bytes=VMEM_LIMIT),
        name="out_ffn",
    )(x, mix, p, wout, g_mlp, w1, w2, g_ple, wg, wple, g_fin)


PROMPT_TILE = 256


def _mix_chunk_steps(u, z, xbc, dt_raw, pos0, keep, const_refs, pool_hist, conv_hist, state_t):
    (wpool_ref, pscale_ref, convw_ref, convb_ref, dtb_ref, alog_ref, dskip_ref, gssm_ref, tri_ref,
     expand_ref) = const_refs
    carried = (lambda v: v) if keep is None else (lambda v: v * keep)

    ext = jnp.concatenate([carried(pool_hist[...]), u], axis=0)
    pool_hist[...] = u[CHUNK - POOL_HIST:, :]
    pos = pos0 + lax.broadcasted_iota(jnp.int32, (CHUNK, 1), 0)
    y_pool = []
    for g, w in enumerate(POOL_WINDOWS):
        cols = slice(g * POOL_GROUP, (g + 1) * POOL_GROUP)
        s = ext[:, cols]
        shift = 1
        while shift < w:
            s = s + pltpu.roll(s, shift, axis=0)
            shift *= 2
        count = jnp.minimum(w, pos + 1).astype(F32)
        d = s[POOL_HIST:, :] / count - u[:, cols]
        y_pool.append(_dot(d, wpool_ref[g]))
    y_pool = jnp.concatenate(y_pool, axis=1) * pscale_ref[...]
    yield MIX_COST_POOL

    extc = jnp.concatenate([carried(conv_hist[...]), xbc], axis=0)
    conv_hist[...] = xbc[CHUNK - CONV_HIST:, :]
    acc = convb_ref[...]
    for k in range(CONV_WIDTH):
        back = CONV_WIDTH - 1 - k
        tap = extc if back == 0 else pltpu.roll(extc, back, axis=0)
        acc = acc + tap[CONV_HIST:, :] * convw_ref[k:k + 1, :]
    xc = _silu(acc)
    xs = xc[:, 0:D_SSM]
    b_all = xc[:, D_SSM:D_SSM + SSM_GROUPS * D_STATE]
    c_all = xc[:, D_SSM + SSM_GROUPS * D_STATE:CONV_DIM]
    yield MIX_COST_CONV

    dt = _softplus(dt_raw + dtb_ref[...])
    a_cum = _chunk_cumsum(tri_ref[...], dt * _neg_exp_heads(alog_ref[...]))
    a_cum_t = a_cum.T
    a_last = a_cum[CHUNK - 1:CHUNK, :]
    expand = expand_ref[...]
    dt_x = _expand_heads(dt, expand)
    ea_x = _expand_heads(jnp.exp(a_cum), expand)
    de_x = _expand_heads(jnp.exp(a_last - a_cum), expand)
    cd_x = _expand_heads(jnp.broadcast_to(jnp.exp(a_last), (2 * SUBLANES, DT_PAD)), expand)[0:1, :]
    x_dt = xs * dt_x
    x_w = x_dt * de_x
    yield MIX_COST_PREP

    row_i = lax.broadcasted_iota(jnp.int32, (CHUNK, CHUNK), 0)
    col_i = lax.broadcasted_iota(jnp.int32, (CHUNK, CHUNK), 1)
    causal = row_i >= col_i
    lane = lax.broadcasted_iota(jnp.int32, (CHUNK, 2 * HEAD_DIM), 1)
    first_head = lane < HEAD_DIM

    y_parts = []
    for g in range(SSM_GROUPS):
        gch = slice(g * GROUP_WIDTH, (g + 1) * GROUP_WIDTH)
        b_g = b_all[:, g * D_STATE:(g + 1) * D_STATE].astype(BF16)
        c_g = c_all[:, g * D_STATE:(g + 1) * D_STATE].astype(BF16)
        cb = lax.dot_general(c_g, b_g, (((1,), (1,)), ((), ())), preferred_element_type=F32)
        st_g = carried(state_t[:, gch])
        y_off = jnp.dot(c_g, st_g.astype(BF16), preferred_element_type=F32) * ea_x[:, gch]
        y_diag = []
        for j in range(HEADS_PER_GROUP // 2):
            h0 = g * HEADS_PER_GROUP + 2 * j
            m = []
            for h in (h0, h0 + 1):
                seg = a_cum[:, h:h + 1] - a_cum_t[h:h + 1, :]
                m.append((cb * jnp.exp(jnp.where(causal, seg, -jnp.inf))).astype(BF16))
            pair = x_dt[:, h0 * HEAD_DIM:(h0 + 2) * HEAD_DIM]
            rhs = jnp.concatenate([jnp.where(first_head, pair, 0.0),
                                   jnp.where(first_head, 0.0, pair)], axis=0).astype(BF16)
            y_diag.append(jnp.dot(jnp.concatenate(m, axis=1), rhs, preferred_element_type=F32))
            yield MIX_COST_HEAD_PAIR
        y_parts.append(jnp.concatenate(y_diag, axis=1) + y_off)
        upd = lax.dot_general(b_g, x_w[:, gch].astype(BF16), (((0,), (0,)), ((), ())),
                              preferred_element_type=F32)
        state_t[:, gch] = st_g * cd_x[:, gch] + upd

    y = jnp.concatenate(y_parts, axis=1) + dskip_ref[...] * xs
    y = y * _silu(z)
    return y_pool, _rms(y, gssm_ref[...])


N_MIX_CONSTS = 10
N_FFN_WEIGHTS = 8


def _prompt_mix_ffn_kernel(u_ref, z_ref, xbc_ref, dt_ref, x_ref, p_ref, *refs, n_tiles, tiles_per_seq):
    const_refs = refs[:N_MIX_CONSTS]
    weight_refs = refs[N_MIX_CONSTS:N_MIX_CONSTS + N_FFN_WEIGHTS]
    y_ref, ssm_ref, mix_scr, pool_hist, conv_hist, state_t = refs[N_MIX_CONSTS + N_FFN_WEIGHTS:]
    s = pl.program_id(0)

    @pl.when(s == 0)
    def _():
        mix_scr[...] = jnp.zeros_like(mix_scr)
        pool_hist[...] = jnp.zeros_like(pool_hist)
        conv_hist[...] = jnp.zeros_like(conv_hist)
        state_t[...] = jnp.zeros_like(state_t)

    tile_in_seq = jnp.minimum(s, n_tiles - 1) % tiles_per_seq
    keep = jnp.where(tile_in_seq == 0, 0.0, 1.0)
    slot = s % 2

    def mixer_steps():
        for ci in range(PROMPT_TILE // CHUNK):
            rows = slice(ci * CHUNK, (ci + 1) * CHUNK)
            y_pool, y_ssm = yield from _mix_chunk_steps(
                u_ref[rows, :], z_ref[rows, :], xbc_ref[rows, :], dt_ref[rows, :],
                tile_in_seq * PROMPT_TILE + ci * CHUNK, keep if ci == 0 else None,
                const_refs, pool_hist, conv_hist, state_t)
            mix_scr[slot, rows, 0:D_POOL] = y_pool.astype(BF16)
            mix_scr[slot, rows, D_POOL:] = y_ssm.astype(BF16)
            yield MIX_COST_TAIL

    ffn_steps = _ffn_tile_steps(x_ref[...], mix_scr[(s + 1) % 2], p_ref[...], *weight_refs)
    mix_total = (PROMPT_TILE // CHUNK) * (MIX_COST_POOL + MIX_COST_CONV + MIX_COST_PREP
                                          + SSM_HEADS // 2 * MIX_COST_HEAD_PAIR + MIX_COST_TAIL)
    y_ref[...] = _interleave(ffn_steps, FFN_COST_TOTAL, mixer_steps(), mix_total)

    @pl.when((tile_in_seq == tiles_per_seq - 1) & (s < n_tiles))
    def _():
        for j in range(D_SSM // LANES):
            ssm_ref[j * LANES:(j + 1) * LANES, :] = state_t[:, j * LANES:(j + 1) * LANES].T


def _prompt_mix_ffn(u, z, xbc, dt, x2d, p2d, consts, ffn_weights, batch, seq):
    tiles_per_seq = seq // PROMPT_TILE
    n_tiles = batch * tiles_per_seq
    mix_row = lambda s: (jnp.minimum(s, n_tiles - 1), 0)
    ffn_row = lambda s: (jnp.maximum(s - 1, 0), 0)
    fixed2 = lambda s: (0, 0)
    fixed3 = lambda s: (0, 0, 0)

    def resident(a):
        return pl.BlockSpec(a.shape, fixed3 if a.ndim == 3 else fixed2, pipeline_mode=pl.Buffered(1))

    return pl.pallas_call(
        functools.partial(_prompt_mix_ffn_kernel, n_tiles=n_tiles, tiles_per_seq=tiles_per_seq),
        grid=(n_tiles + 1,),
        in_specs=[
            pl.BlockSpec((PROMPT_TILE, D_POOL), mix_row),
            pl.BlockSpec((PROMPT_TILE, D_SSM), mix_row),
            pl.BlockSpec((PROMPT_TILE, CONV_DIM), mix_row),
            pl.BlockSpec((PROMPT_TILE, DT_PAD), mix_row),
            pl.BlockSpec((PROMPT_TILE, D_MODEL), ffn_row),
            pl.BlockSpec((PROMPT_TILE, D_PLE), ffn_row),
        ] + [resident(a) for a in consts] + [resident(a) for a in ffn_weights],
        out_specs=[
            pl.BlockSpec((PROMPT_TILE, D_MODEL), ffn_row),
            pl.BlockSpec((None, D_SSM, D_STATE),
                         lambda s: (jnp.minimum(s, n_tiles - 1) // tiles_per_seq, 0, 0)),
        ],
        out_shape=[
            jax.ShapeDtypeStruct((batch * seq, D_MODEL), F32),
            jax.ShapeDtypeStruct((batch, D_SSM, D_STATE), F32),
        ],
        scratch_shapes=[
            pltpu.VMEM((2, PROMPT_TILE, 2 * D_MODEL), BF16),
            pltpu.VMEM((POOL_HIST, D_POOL), F32),
            pltpu.VMEM((CONV_HIST, CONV_DIM), F32),
            pltpu.VMEM((D_STATE, D_SSM), F32),
        ],
        compiler_params=pltpu.CompilerParams(
            dimension_semantics=("arbitrary",), vmem_limit_bytes=VMEM_LIMIT),
        name="prompt_mix_ffn",
    )(u, z, xbc, dt, x2d, p2d, *consts, *ffn_weights)


SAMPLE_T = 4
SAMPLE_BB = 8


def _sample_mix_kernel(u_ref, z_ref, xbc_ref, dt_ref, pbuf_ref, cbuf_ref, state_ref,
                       wpool_ref, pscale_ref, convw_ref, convb_ref, dtb_ref, alog_ref, dskip_ref,
                       gssm_ref, expand_ref,
                       mix_ref, npool_ref, nconv_ref, nstate_ref):
    nt, bb = SAMPLE_T, SAMPLE_BB
    rows = nt * bb

    at_time = lambda ref, t: ref[:, t, :]
    ext = [pbuf_ref[i] for i in range(POOL_BUF)] + [at_time(u_ref, t) for t in range(nt)]
    for i in range(POOL_BUF):
        npool_ref[i] = ext[i + nt]
    y_pool = []
    for g, w in enumerate(POOL_WINDOWS):
        cols = slice(g * POOL_GROUP, (g + 1) * POOL_GROUP)
        d = []
        for t in range(nt):
            hi = POOL_BUF + t
            win = ext[hi - w + 1][:, cols]
            for i in range(hi - w + 2, hi + 1):
                win = win + ext[i][:, cols]
            count = float(min(w, PAST_LEN + t + 1))
            d.append(win / count - ext[hi][:, cols])
        y_pool.append(_dot(jnp.concatenate(d, axis=0), wpool_ref[g]))
    y_pool = jnp.concatenate(y_pool, axis=1) * pscale_ref[...]

    extc = [cbuf_ref[i] for i in range(CONV_WIDTH - 1)] + [at_time(xbc_ref, t) for t in range(nt)]
    for j in range(CONV_WIDTH - 1):
        nconv_ref[j] = extc[nt + j]
    xc = []
    for t in range(nt):
        acc = convb_ref[...]
        for k in range(CONV_WIDTH):
            acc = acc + extc[t + k] * convw_ref[k:k + 1, :]
        xc.append(_silu(acc))
    xs = [v[:, 0:D_SSM] for v in xc]
    bm = [v[:, D_SSM:D_SSM + SSM_GROUPS * D_STATE] for v in xc]
    cm = [v[:, D_SSM + SSM_GROUPS * D_STATE:CONV_DIM] for v in xc]

    a_row = _neg_exp_heads(alog_ref[...])
    dt = [_softplus(at_time(dt_ref, t) + dtb_ref[...]) for t in range(nt)]
    a_cum = []
    for t in range(nt):
        a_t = dt[t] * a_row
        a_cum.append(a_t if t == 0 else a_cum[-1] + a_t)
    lane = lax.broadcasted_iota(jnp.int32, (bb, DT_PAD), 1)
    in_group0 = lane < HEADS_PER_GROUP
    head_rows = []
    pair_index = {}
    for l in range(nt):
        for s in range(l + 1):
            cb = [jnp.sum(cm[l][:, g * D_STATE:(g + 1) * D_STATE] * bm[s][:, g * D_STATE:(g + 1) * D_STATE],
                          axis=-1, keepdims=True) for g in range(SSM_GROUPS)]
            cb_heads = jnp.where(in_group0, cb[0], cb[1])
            pair_index[(l, s)] = len(head_rows)
            head_rows.append(cb_heads * jnp.exp(a_cum[l] - a_cum[s]) * dt[s])
    ea_index = len(head_rows)
    head_rows += [jnp.exp(a_cum[t]) for t in range(nt)]
    w_index = len(head_rows)
    head_rows += [jnp.exp(a_cum[nt - 1] - a_cum[t]) * dt[t] for t in range(nt)]
    cd_index = len(head_rows)
    head_rows.append(jnp.exp(a_cum[nt - 1]))
    chan = _expand_heads(jnp.concatenate(head_rows, axis=0), expand_ref[...])
    chan_rows = lambda i: chan[i * bb:(i + 1) * bb, :]

    y_diag = []
    for l in range(nt):
        acc = chan_rows(pair_index[(l, 0)]) * xs[0]
        for s in range(1, l + 1):
            acc = acc + chan_rows(pair_index[(l, s)]) * xs[s]
        y_diag.append(acc)
    xs_all = jnp.concatenate(xs, axis=0)
    xw_all = jnp.concatenate([chan_rows(w_index + t) * xs[t] for t in range(nt)], axis=0)
    ea_all = jnp.concatenate([chan_rows(ea_index + t) for t in range(nt)], axis=0)
    b_all = jnp.concatenate(bm, axis=0).astype(BF16)
    c_all = jnp.concatenate(cm, axis=0).astype(BF16)
    cd3 = jnp.concatenate([p.astype(F32) for p in _split3(chan_rows(cd_index))], axis=0)
    ones3 = jnp.ones((3 * bb, D_STATE), BF16)
    row_batch = lax.broadcasted_iota(jnp.int32, (rows, 1), 0) & (bb - 1)
    row_batch3 = lax.broadcasted_iota(jnp.int32, (3 * bb, 1), 0) & (bb - 1)
    contract_rows = (((0,), (0,)), ((), ()))

    def per_batch(b, y_off):
        st = state_ref[b]
        st16 = st.astype(BF16)
        mine = row_batch == b
        xw_b = jnp.where(mine, xw_all, 0.0).astype(BF16)
        cd_b = jnp.where(row_batch3 == b, cd3, 0.0).astype(BF16)
        scale = lax.dot_general(cd_b, ones3, contract_rows, preferred_element_type=F32)
        yo = []
        for g in range(SSM_GROUPS):
            grow = slice(g * GROUP_WIDTH, (g + 1) * GROUP_WIDTH)
            gst = slice(g * D_STATE, (g + 1) * D_STATE)
            yo.append(lax.dot_general(c_all[:, gst], st16[grow, :], (((1,), (1,)), ((), ())),
                                      preferred_element_type=F32))
            upd = lax.dot_general(xw_b[:, grow], b_all[:, gst], contract_rows,
                                  preferred_element_type=F32)
            nstate_ref[b, grow, :] = st[grow, :] * scale[grow, :] + upd
        return jnp.where(mine, jnp.concatenate(yo, axis=1), y_off)

    y_off = lax.fori_loop(0, bb, per_batch, jnp.zeros((rows, D_SSM), F32), unroll=True)
    y = jnp.concatenate(y_diag, axis=0) + y_off * ea_all + dskip_ref[...] * xs_all
    y = y * _silu(jnp.concatenate([at_time(z_ref, t) for t in range(nt)], axis=0))
    y_ssm = _rms(y, gssm_ref[...])
    for t in range(nt):
        mix_ref[:, t, 0:D_POOL] = y_pool[t * bb:(t + 1) * bb, :].astype(mix_ref.dtype)
        mix_ref[:, t, D_POOL:] = y_ssm[t * bb:(t + 1) * bb, :].astype(mix_ref.dtype)


def _sample_group_mix(u, z, xbc, dt, pbuf, cbuf, state, consts):
    nt, bb = SAMPLE_T, SAMPLE_BB
    batch = state.shape[0]
    tblk = lambda i: (0, i, 0)
    (wpool, pscale, convw, convb, dtb, alog, dskip_x, gssm, _, expand) = consts
    blk3 = lambda i: (i, 0, 0)
    fixed2 = lambda i: (0, 0)
    fixed3 = lambda i: (0, 0, 0)
    return pl.pallas_call(
        _sample_mix_kernel,
        grid=(batch // bb,),
        in_specs=[
            pl.BlockSpec((bb, nt, D_POOL), blk3),
            pl.BlockSpec((bb, nt, D_SSM), blk3),
            pl.BlockSpec((bb, nt, CONV_DIM), blk3),
            pl.BlockSpec((bb, nt, DT_PAD), blk3),
            pl.BlockSpec((POOL_BUF, bb, D_POOL), tblk),
            pl.BlockSpec((CONV_WIDTH - 1, bb, CONV_DIM), tblk),
            pl.BlockSpec((bb, D_SSM, D_STATE), blk3),
            pl.BlockSpec((len(POOL_WINDOWS), POOL_GROUP, POOL_GROUP), fixed3),
            pl.BlockSpec((1, D_POOL), fixed2),
            pl.BlockSpec((CONV_WIDTH, CONV_DIM), fixed2),
            pl.BlockSpec((1, CONV_DIM), fixed2),
            pl.BlockSpec((1, DT_PAD), fixed2),
            pl.BlockSpec((1, DT_PAD), fixed2),
            pl.BlockSpec((1, D_SSM), fixed2),
            pl.BlockSpec((1, D_SSM), fixed2),
            pl.BlockSpec((DT_PAD, D_SSM), fixed2),
        ],
        out_specs=[
            pl.BlockSpec((bb, nt, 2 * D_MODEL), blk3),
            pl.BlockSpec((POOL_BUF, bb, D_POOL), tblk),
            pl.BlockSpec((CONV_WIDTH - 1, bb, CONV_DIM), tblk),
            pl.BlockSpec((bb, D_SSM, D_STATE), blk3),
        ],
        out_shape=[
            jax.ShapeDtypeStruct((batch, nt, 2 * D_MODEL), F32),
            jax.ShapeDtypeStruct((POOL_BUF, batch, D_POOL), F32),
            jax.ShapeDtypeStruct((CONV_WIDTH - 1, batch, CONV_DIM), F32),
            jax.ShapeDtypeStruct((batch, D_SSM, D_STATE), F32),
        ],
        compiler_params=pltpu.CompilerParams(
            dimension_semantics=("arbitrary",), vmem_limit_bytes=VMEM_LIMIT),
        name="sample_mix",
    )(u, z, xbc, dt, pbuf, cbuf, state, wpool, pscale, convw, convb, dtb, alog, dskip_x, gssm, expand)


def _mix_constants(w_pool, pool_scale, conv_w, conv_b, dt_bias, a_log, d_skip, ssm_norm_g):
    pad = DT_PAD - SSM_HEADS
    head_of_channel = jnp.arange(D_SSM, dtype=jnp.int32) // HEAD_DIM
    slot_row = jnp.arange(DT_PAD, dtype=jnp.int32)
    expand = ((slot_row[:, None] % HEAD_SLOT == head_of_channel[None, :])
              & (slot_row[:, None] < 3 * HEAD_SLOT)).astype(BF16)
    tri = (jnp.arange(CHUNK)[:, None] >= jnp.arange(CHUNK)[None, :]).astype(BF16)
    return (
        w_pool.astype(BF16),
        pool_scale.reshape(1, D_POOL),
        conv_w,
        conv_b.reshape(1, CONV_DIM),
        jnp.pad(dt_bias, (0, pad)).reshape(1, DT_PAD),
        jnp.pad(a_log, (0, pad)).reshape(1, DT_PAD),
        jnp.repeat(d_skip, HEAD_DIM).reshape(1, D_SSM),
        ssm_norm_g.reshape(1, D_SSM),
        tri,
        expand,
    )


PROMPT_IN_TM = 512
SAMPLE_TB = 64


def kernel(x_prompt, x_sample, p_prompt, p_sample, state_pool, state_conv, state_ssm, w_in, w_pool, pool_scale, conv_w, conv_b, dt_bias, a_log, d_skip, ssm_norm_g, w_out, norm_mix_g, norm_mlp_g, w_ff1, w_ff2, norm_ple_g, w_gate, w_ple, final_norm_g):
    assert w_in.shape[0] == 1, "single-layer trunk"
    bp, tp = x_prompt.shape[:2]
    bs, ts = x_sample.shape[:2]
    assert tp % PROMPT_TILE == 0 and tp >= POOL_BUF and ts == SAMPLE_T and bs % SAMPLE_BB == 0

    row = lambda v: v.reshape(1, -1)
    w_in_t = jnp.pad(w_in[0].T, ((0, D_IN_PAD - w_in.shape[2]), (0, 0))).astype(BF16)
    consts = _mix_constants(w_pool[0], pool_scale[0], conv_w[0], conv_b[0], dt_bias[0], a_log[0],
                            d_skip[0], ssm_norm_g[0])
    ffn_weights = (w_out[0].astype(BF16), row(norm_mlp_g[0]), w_ff1[0].astype(BF16),
                   w_ff2[0].astype(BF16), row(norm_ple_g[0]), w_gate[0].astype(BF16),
                   w_ple[0].astype(BF16), row(final_norm_g))
    g_mix = row(norm_mix_g[0])

    xp = x_prompt.reshape(bp * tp, D_MODEL)
    u, z, xbc, dt = _in_proj(xp, g_mix, w_in_t, PROMPT_IN_TM)
    y_prompt, ssm_p = _prompt_mix_ffn(u, z, xbc, dt, xp, p_prompt[0].reshape(bp * tp, D_PLE), consts,
                                      ffn_weights, bp, tp)
    pool_p = u.reshape(bp, tp, D_POOL)[:, tp - POOL_BUF:, :]
    conv_p = xbc.reshape(bp, tp, CONV_DIM)[:, tp - (CONV_WIDTH - 1):, :]

    tmaj = lambda a: jnp.transpose(a, (1, 0, 2))
    u_s, z_s, xbc_s, dt_s = _in_proj(x_sample, g_mix, w_in_t, SAMPLE_TB)
    mix_s, pool_s, conv_s, ssm_s = _sample_group_mix(
        u_s, z_s, xbc_s, dt_s, tmaj(state_pool[0]), tmaj(state_conv[0]),
        state_ssm[0].reshape(bs, D_SSM, D_STATE), consts)
    y_sample = _out_ffn(x_sample, mix_s, p_sample[0], *ffn_weights, SAMPLE_TB)

    state_shape = (1, -1, SSM_HEADS, HEAD_DIM, D_STATE)
    return (y_prompt.reshape(bp, tp, D_MODEL), y_sample,
            pool_p[None], conv_p[None], ssm_p.reshape(state_shape),
            tmaj(pool_s)[None], tmaj(conv_s)[None], ssm_s.reshape(state_shape))
```

```python
import functools
import math

import jax
import jax.numpy as jnp
from jax import lax
from jax.experimental import pallas as pl
from jax.experimental.pallas import tpu as pltpu

F32 = jnp.float32
BF16 = jnp.bfloat16

D_MODEL = 1024
D_POOL = 1024
D_SSM = 1024
POOL_WINDOWS = (2, 4, 8, 16)
POOL_GROUP = 256
POOL_BUF = 15
SSM_HEADS = 16
HEAD_DIM = 64
SSM_GROUPS = 2
HEADS_PER_GROUP = 8
GROUP_WIDTH = HEADS_PER_GROUP * HEAD_DIM
D_STATE = 128
CONV_WIDTH = 4
CONV_DIM = D_SSM + 2 * SSM_GROUPS * D_STATE
CHUNK = 128
D_FF = 4096
D_PLE = 256
PAST_LEN = 16384
EPS = 1e-6

LANES = 128
SUBLANES = 8
DT_PAD = LANES
HEAD_SLOT = SSM_HEADS
D_IN_PAD = D_POOL + D_SSM + CONV_DIM + DT_PAD
POOL_HIST = 16
CONV_HIST = SUBLANES
VMEM_LIMIT = 56 * 1024 * 1024
FFN_STAGES = 4


def _rms(x, g):
    return x * lax.rsqrt(jnp.mean(x * x, axis=-1, keepdims=True) + EPS) * g


def _silu(x):
    return x * jax.nn.sigmoid(x)


def _softplus(x):
    return jnp.maximum(x, 0.0) + jnp.log1p(jnp.exp(-jnp.abs(x)))


def _neg_exp_heads(a_log_row):
    lane = lax.broadcasted_iota(jnp.int32, a_log_row.shape, a_log_row.ndim - 1)
    return jnp.where(lane < SSM_HEADS, -jnp.exp(a_log_row), 0.0)


def _load_rows(ref):
    if len(ref.shape) == 2:
        return ref[...]
    return jnp.concatenate([ref[:, t, :] for t in range(ref.shape[1])], axis=0)


def _store_rows(ref, val):
    if len(ref.shape) == 2:
        ref[...] = val.astype(ref.dtype)
    else:
        nb = ref.shape[0]
        for t in range(ref.shape[1]):
            ref[:, t, :] = val[t * nb:(t + 1) * nb, :].astype(ref.dtype)


def _dot(a, b):
    return jnp.dot(a.astype(BF16), b.astype(BF16), preferred_element_type=F32)


def _split3(x):
    hi = x.astype(BF16)
    r1 = x - hi.astype(F32)
    mid = r1.astype(BF16)
    lo = (r1 - mid.astype(F32)).astype(BF16)
    return hi, mid, lo


def _pack_heads(v):
    lane = lax.broadcasted_iota(jnp.int32, v.shape, 1)
    hi, mid, lo = _split3(jnp.where(lane < SSM_HEADS, v, 0.0))
    packed = (hi.astype(F32) + pltpu.roll(mid.astype(F32), HEAD_SLOT, axis=1)
              + pltpu.roll(lo.astype(F32), 2 * HEAD_SLOT, axis=1))
    return packed.astype(BF16)


def _expand_heads(v, expand3):
    return jnp.dot(_pack_heads(v), expand3, preferred_element_type=F32)


def _chunk_cumsum(tri16, a):
    s = jnp.dot(tri16, _pack_heads(a), preferred_element_type=F32)
    return s + pltpu.roll(s, DT_PAD - HEAD_SLOT, axis=1) + pltpu.roll(s, DT_PAD - 2 * HEAD_SLOT, axis=1)


W_IN_CAST_ROWS = 240


def _in_proj_kernel(x_ref, g_ref, w_ref, u_ref, z_ref, xbc_ref, dt_ref, *wt16_out, n_in):
    if wt16_out:
        (wt_ref,) = wt16_out

        @pl.when(pl.program_id(0) == 0)
        def _():
            def cast_rows(k, carry):
                rows = pl.ds(pl.multiple_of(k * W_IN_CAST_ROWS, W_IN_CAST_ROWS), W_IN_CAST_ROWS)
                wt_ref[rows, :] = w_ref[rows, :].astype(BF16)
                return carry
            lax.fori_loop(0, n_in // W_IN_CAST_ROWS, cast_rows, 0)
            wt_ref[n_in:, :] = jnp.zeros((D_IN_PAD - n_in, D_MODEL), BF16)
    else:
        wt_ref = w_ref
    xn = _rms(_load_rows(x_ref), g_ref[...]).astype(BF16)
    start = 0
    for out_ref, width in ((u_ref, D_POOL), (z_ref, D_SSM), (xbc_ref, CONV_DIM), (dt_ref, DT_PAD)):
        _store_rows(out_ref, lax.dot_general(xn, wt_ref[start:start + width, :], (((1,), (1,)), ((), ())),
                                             preferred_element_type=F32))
        start += width


def _row_spec(a_shape, tm, width):
    if len(a_shape) == 2:
        return pl.BlockSpec((tm, width), lambda i: (i, 0))
    return pl.BlockSpec((tm, a_shape[1], width), lambda i: (i, 0, 0))


def _in_proj(x, g, w_in_t, tm):
    lead = x.shape[:-1]
    fixed = lambda i: (0, 0)
    widths = (D_POOL, D_SSM, CONV_DIM, DT_PAD)
    cast_here = w_in_t.dtype != BF16
    n_in = w_in_t.shape[0]
    assert (not cast_here and n_in == D_IN_PAD) or (n_in % W_IN_CAST_ROWS == 0 and n_in % (2 * SUBLANES) == 0)
    out_specs = [_row_spec(x.shape, tm, w) for w in widths]
    out_shape = [jax.ShapeDtypeStruct(lead + (w,), F32) for w in widths]
    if cast_here:
        out_specs.append(pl.BlockSpec((D_IN_PAD, D_MODEL), fixed))
        out_shape.append(jax.ShapeDtypeStruct((D_IN_PAD, D_MODEL), BF16))
    return pl.pallas_call(
        functools.partial(_in_proj_kernel, n_in=n_in),
        grid=(lead[0] // tm,),
        in_specs=[
            _row_spec(x.shape, tm, D_MODEL),
            pl.BlockSpec((1, D_MODEL), fixed),
            pl.BlockSpec(w_in_t.shape, fixed, pipeline_mode=pl.Buffered(1)),
        ],
        out_specs=out_specs,
        out_shape=out_shape,
        compiler_params=pltpu.CompilerParams(
            dimension_semantics=("arbitrary",), vmem_limit_bytes=VMEM_LIMIT),
        name="in_proj",
    )(x, g, w_in_t)


def _run(steps):
    try:
        while True:
            next(steps)
    except StopIteration as done:
        return done.value


def _interleave(main, side, side_per_main):
    result = None
    main_done = side_done = False
    while not (main_done and side_done):
        if not main_done:
            try:
                next(main)
            except StopIteration as done:
                result, main_done = done.value, True
        for _ in range(side_per_main):
            if side_done:
                break
            try:
                next(side)
            except StopIteration:
                side_done = True
    return result


def _ffn_tile_steps(x, mix16, p, wout_ref, g_mlp_ref, w1_ref, w2_ref, g_ple_ref, wg_ref, wple_ref, g_fin_ref):
    h = x + jnp.dot(mix16, wout_ref[...], preferred_element_type=F32)
    yield
    hn = _rms(h, g_mlp_ref[...]).astype(BF16)
    ff_block = D_FF // FFN_STAGES
    for c in range(FFN_STAGES):
        f = jnp.dot(hn, w1_ref[:, c * ff_block:(c + 1) * ff_block], preferred_element_type=F32)
        f = jnp.square(jnp.maximum(f, 0.0)).astype(BF16)
        h = h + jnp.dot(f, w2_ref[c * ff_block:(c + 1) * ff_block, :], preferred_element_type=F32)
        yield
    gate = jax.nn.sigmoid(
        jnp.dot(_rms(h, g_ple_ref[...]).astype(BF16), wg_ref[...], preferred_element_type=F32))
    h = h + gate * jnp.dot(p.astype(BF16), wple_ref[...], preferred_element_type=F32)
    return _rms(h, g_fin_ref[...])


def _out_ffn_kernel(x_ref, mix_ref, p_ref, *refs):
    *weight_refs, y_ref = refs
    _store_rows(y_ref, _run(_ffn_tile_steps(_load_rows(x_ref), _load_rows(mix_ref).astype(BF16),
                                            _load_rows(p_ref), *weight_refs)))


def _out_ffn(x, mix, p, wout, g_mlp, w1, w2, g_ple, wg, wple, g_fin, tm):
    fixed = lambda i: (0, 0)

    def resident(shape):
        return pl.BlockSpec(shape, fixed, pipeline_mode=pl.Buffered(1))

    return pl.pallas_call(
        _out_ffn_kernel,
        grid=(x.shape[0] // tm,),
        in_specs=[
            _row_spec(x.shape, tm, D_MODEL),
            _row_spec(x.shape, tm, 2 * D_MODEL),
            _row_spec(x.shape, tm, D_PLE),
            resident((2 * D_MODEL, D_MODEL)),
            resident((1, D_MODEL)),
            resident((D_MODEL, D_FF)),
            resident((D_FF, D_MODEL)),
            resident((1, D_MODEL)),
            resident((D_MODEL, D_MODEL)),
            resident((D_PLE, D_MODEL)),
            resident((1, D_MODEL)),
        ],
        out_specs=_row_spec(x.shape, tm, D_MODEL),
        out_shape=jax.ShapeDtypeStruct(x.shape, F32),
        compiler_params=pltpu.CompilerParams(
            dimension_semantics=("arbitrary",), vmem_limit_bytes=VMEM_LIMIT),
        name="out_ffn",
    )(x, mix, p, wout, g_mlp, w1, w2, g_ple, wg, wple, g_fin)


PROMPT_TILE = 256


def _mix_chunk_steps(u, z, xbc, dt_raw, pos0, keep, const_refs, pool_hist, conv_hist, state_t):
    (wpool_ref, pscale_ref, convw_ref, convb_ref, dtb_ref, alog_ref, dskip_ref, gssm_ref, tri_ref,
     expand_ref) = const_refs
    carried = (lambda v: v) if keep is None else (lambda v: v * keep)

    ext = jnp.concatenate([carried(pool_hist[...]), u], axis=0)
    pool_hist[...] = u[CHUNK - POOL_HIST:, :]
    pos = pos0 + lax.broadcasted_iota(jnp.int32, (CHUNK, 1), 0)
    y_pool = []
    for g, w in enumerate(POOL_WINDOWS):
        cols = slice(g * POOL_GROUP, (g + 1) * POOL_GROUP)
        s = ext[:, cols]
        shift = 1
        while shift < w:
            s = s + pltpu.roll(s, shift, axis=0)
            shift *= 2
        count = jnp.minimum(w, pos + 1).astype(F32)
        d = s[POOL_HIST:, :] / count - u[:, cols]
        y_pool.append(_dot(d, wpool_ref[g]))
    y_pool = jnp.concatenate(y_pool, axis=1) * pscale_ref[...]
    yield

    extc = jnp.concatenate([carried(conv_hist[...]), xbc], axis=0)
    conv_hist[...] = xbc[CHUNK - CONV_HIST:, :]
    acc = convb_ref[...]
    for k in range(CONV_WIDTH):
        back = CONV_WIDTH - 1 - k
        tap = extc if back == 0 else pltpu.roll(extc, back, axis=0)
        acc = acc + tap[CONV_HIST:, :] * convw_ref[k:k + 1, :]
    xc = _silu(acc)
    xs = xc[:, 0:D_SSM]
    b_all = xc[:, D_SSM:D_SSM + SSM_GROUPS * D_STATE]
    c_all = xc[:, D_SSM + SSM_GROUPS * D_STATE:CONV_DIM]
    yield

    dt = _softplus(dt_raw + dtb_ref[...])
    a_cum = _chunk_cumsum(tri_ref[...], dt * _neg_exp_heads(alog_ref[...]))
    a_cum_t = a_cum.T
    a_last = a_cum[CHUNK - 1:CHUNK, :]
    expand = expand_ref[...]
    dt_x = _expand_heads(dt, expand)
    ea_x = _expand_heads(jnp.exp(a_cum), expand)
    de_x = _expand_heads(jnp.exp(a_last - a_cum), expand)
    cd_x = _expand_heads(jnp.broadcast_to(jnp.exp(a_last), (2 * SUBLANES, DT_PAD)), expand)[0:1, :]
    x_dt = xs * dt_x
    x_w = x_dt * de_x
    yield

    row_i = lax.broadcasted_iota(jnp.int32, (CHUNK, CHUNK), 0)
    col_i = lax.broadcasted_iota(jnp.int32, (CHUNK, CHUNK), 1)
    causal = row_i >= col_i
    lane = lax.broadcasted_iota(jnp.int32, (CHUNK, 2 * HEAD_DIM), 1)
    first_head = lane < HEAD_DIM

    y_parts = []
    for g in range(SSM_GROUPS):
        gch = slice(g * GROUP_WIDTH, (g + 1) * GROUP_WIDTH)
        b_g = b_all[:, g * D_STATE:(g + 1) * D_STATE].astype(BF16)
        c_g = c_all[:, g * D_STATE:(g + 1) * D_STATE].astype(BF16)
        cb = lax.dot_general(c_g, b_g, (((1,), (1,)), ((), ())), preferred_element_type=F32)
        st_g = carried(state_t[:, gch])
        y_off = jnp.dot(c_g, st_g.astype(BF16), preferred_element_type=F32) * ea_x[:, gch]
        y_diag = []
        for j in range(HEADS_PER_GROUP // 2):
            h0 = g * HEADS_PER_GROUP + 2 * j
            m = []
            for h in (h0, h0 + 1):
                seg = a_cum[:, h:h + 1] - a_cum_t[h:h + 1, :]
                m.append((cb * jnp.exp(jnp.where(causal, seg, -jnp.inf))).astype(BF16))
            pair = x_dt[:, h0 * HEAD_DIM:(h0 + 2) * HEAD_DIM]
            rhs = jnp.concatenate([jnp.where(first_head, pair, 0.0),
                                   jnp.where(first_head, 0.0, pair)], axis=0).astype(BF16)
            y_diag.append(jnp.dot(jnp.concatenate(m, axis=1), rhs, preferred_element_type=F32))
            yield
        y_parts.append(jnp.concatenate(y_diag, axis=1) + y_off)
        upd = lax.dot_general(b_g, x_w[:, gch].astype(BF16), (((0,), (0,)), ((), ())),
                              preferred_element_type=F32)
        state_t[:, gch] = st_g * cd_x[:, gch] + upd

    y = jnp.concatenate(y_parts, axis=1) + dskip_ref[...] * xs
    y = y * _silu(z)
    return y_pool, _rms(y, gssm_ref[...])


N_MIX_CONSTS = 10
N_FFN_WEIGHTS = 8
MIXER_STAGES_PER_FFN_STAGE = 4


def _prompt_mix_ffn_kernel(u_ref, z_ref, xbc_ref, dt_ref, x_ref, p_ref, *refs, n_tiles, tiles_per_seq):
    const_refs = refs[:N_MIX_CONSTS]
    weight_refs = refs[N_MIX_CONSTS:N_MIX_CONSTS + N_FFN_WEIGHTS]
    y_ref, ssm_ref, mix_scr, pool_hist, conv_hist, state_t = refs[N_MIX_CONSTS + N_FFN_WEIGHTS:]
    s = pl.program_id(0)

    @pl.when(s == 0)
    def _():
        mix_scr[...] = jnp.zeros_like(mix_scr)
        pool_hist[...] = jnp.zeros_like(pool_hist)
        conv_hist[...] = jnp.zeros_like(conv_hist)
        state_t[...] = jnp.zeros_like(state_t)

    tile_in_seq = jnp.minimum(s, n_tiles - 1) % tiles_per_seq
    keep = jnp.where(tile_in_seq == 0, 0.0, 1.0)
    slot = s % 2

    def mixer_steps():
        for ci in range(PROMPT_TILE // CHUNK):
            rows = slice(ci * CHUNK, (ci + 1) * CHUNK)
            y_pool, y_ssm = yield from _mix_chunk_steps(
                u_ref[rows, :], z_ref[rows, :], xbc_ref[rows, :], dt_ref[rows, :],
                tile_in_seq * PROMPT_TILE + ci * CHUNK, keep if ci == 0 else None,
                const_refs, pool_hist, conv_hist, state_t)
            mix_scr[slot, rows, 0:D_POOL] = y_pool.astype(BF16)
            mix_scr[slot, rows, D_POOL:] = y_ssm.astype(BF16)
            yield

    ffn_steps = _ffn_tile_steps(x_ref[...], mix_scr[(s + 1) % 2], p_ref[...], *weight_refs)
    y_ref[...] = _interleave(ffn_steps, mixer_steps(), MIXER_STAGES_PER_FFN_STAGE)

    @pl.when((tile_in_seq == tiles_per_seq - 1) & (s < n_tiles))
    def _():
        for j in range(D_SSM // LANES):
            ssm_ref[j * LANES:(j + 1) * LANES, :] = state_t[:, j * LANES:(j + 1) * LANES].T


def _prompt_mix_ffn(u, z, xbc, dt, x2d, p2d, consts, ffn_weights, batch, seq):
    tiles_per_seq = seq // PROMPT_TILE
    n_tiles = batch * tiles_per_seq
    mix_row = lambda s: (jnp.minimum(s, n_tiles - 1), 0)
    ffn_row = lambda s: (jnp.maximum(s - 1, 0), 0)
    fixed2 = lambda s: (0, 0)
    fixed3 = lambda s: (0, 0, 0)

    def resident(a):
        return pl.BlockSpec(a.shape, fixed3 if a.ndim == 3 else fixed2, pipeline_mode=pl.Buffered(1))

    return pl.pallas_call(
        functools.partial(_prompt_mix_ffn_kernel, n_tiles=n_tiles, tiles_per_seq=tiles_per_seq),
        grid=(n_tiles + 1,),
        in_specs=[
            pl.BlockSpec((PROMPT_TILE, D_POOL), mix_row),
            pl.BlockSpec((PROMPT_TILE, D_SSM), mix_row),
            pl.BlockSpec((PROMPT_TILE, CONV_DIM), mix_row),
            pl.BlockSpec((PROMPT_TILE, DT_PAD), mix_row),
            pl.BlockSpec((PROMPT_TILE, D_MODEL), ffn_row),
            pl.BlockSpec((PROMPT_TILE, D_PLE), ffn_row),
        ] + [resident(a) for a in consts] + [resident(a) for a in ffn_weights],
        out_specs=[
            pl.BlockSpec((PROMPT_TILE, D_MODEL), ffn_row),
            pl.BlockSpec((None, D_SSM, D_STATE),
                         lambda s: (jnp.minimum(s, n_tiles - 1) // tiles_per_seq, 0, 0)),
        ],
        out_shape=[
            jax.ShapeDtypeStruct((batch * seq, D_MODEL), F32),
            jax.ShapeDtypeStruct((batch, D_SSM, D_STATE), F32),
        ],
        scratch_shapes=[
            pltpu.VMEM((2, PROMPT_TILE, 2 * D_MODEL), BF16),
            pltpu.VMEM((POOL_HIST, D_POOL), F32),
            pltpu.VMEM((CONV_HIST, CONV_DIM), F32),
            pltpu.VMEM((D_STATE, D_SSM), F32),
        ],
        compiler_params=pltpu.CompilerParams(
            dimension_semantics=("arbitrary",), vmem_limit_bytes=VMEM_LIMIT),
        name="prompt_mix_ffn",
    )(u, z, xbc, dt, x2d, p2d, *consts, *ffn_weights)


SAMPLE_T = 4
SAMPLE_BB = 8


def _sample_mix_kernel(u_ref, z_ref, xbc_ref, dt_ref, pbuf_ref, cbuf_ref, state_ref,
                       wpool_ref, pscale_ref, convw_ref, convb_ref, dtb_ref, alog_ref, dskip_ref,
                       gssm_ref, expand_ref,
                       mix_ref, npool_ref, nconv_ref, nstate_ref):
    nt, bb = SAMPLE_T, SAMPLE_BB
    rows = nt * bb

    at_time = lambda ref, t: ref[:, t, :]
    ext = [pbuf_ref[i] for i in range(POOL_BUF)] + [at_time(u_ref, t) for t in range(nt)]
    for i in range(POOL_BUF):
        npool_ref[i] = ext[i + nt]
    y_pool = []
    for g, w in enumerate(POOL_WINDOWS):
        cols = slice(g * POOL_GROUP, (g + 1) * POOL_GROUP)
        d = []
        for t in range(nt):
            hi = POOL_BUF + t
            win = ext[hi - w + 1][:, cols]
            for i in range(hi - w + 2, hi + 1):
                win = win + ext[i][:, cols]
            count = float(min(w, PAST_LEN + t + 1))
            d.append(win / count - ext[hi][:, cols])
        y_pool.append(_dot(jnp.concatenate(d, axis=0), wpool_ref[g]))
    y_pool = jnp.concatenate(y_pool, axis=1) * pscale_ref[...]

    extc = [cbuf_ref[i] for i in range(CONV_WIDTH - 1)] + [at_time(xbc_ref, t) for t in range(nt)]
    for j in range(CONV_WIDTH - 1):
        nconv_ref[j] = extc[nt + j]
    xc = []
    for t in range(nt):
        acc = convb_ref[...]
        for k in range(CONV_WIDTH):
            acc = acc + extc[t + k] * convw_ref[k:k + 1, :]
        xc.append(_silu(acc))
    xs = [v[:, 0:D_SSM] for v in xc]
    bm = [v[:, D_SSM:D_SSM + SSM_GROUPS * D_STATE] for v in xc]
    cm = [v[:, D_SSM + SSM_GROUPS * D_STATE:CONV_DIM] for v in xc]

    a_row = _neg_exp_heads(alog_ref[...])
    dt = [_softplus(at_time(dt_ref, t) + dtb_ref[...]) for t in range(nt)]
    a_cum = []
    for t in range(nt):
        a_t = dt[t] * a_row
        a_cum.append(a_t if t == 0 else a_cum[-1] + a_t)
    lane = lax.broadcasted_iota(jnp.int32, (bb, DT_PAD), 1)
    in_group0 = lane < HEADS_PER_GROUP
    head_rows = []
    pair_index = {}
    for l in range(nt):
        for s in range(l + 1):
            cb = [jnp.sum(cm[l][:, g * D_STATE:(g + 1) * D_STATE] * bm[s][:, g * D_STATE:(g + 1) * D_STATE],
                          axis=-1, keepdims=True) for g in range(SSM_GROUPS)]
            cb_heads = jnp.where(in_group0, cb[0], cb[1])
            pair_index[(l, s)] = len(head_rows)
            head_rows.append(cb_heads * jnp.exp(a_cum[l] - a_cum[s]) * dt[s])
    ea_index = len(head_rows)
    head_rows += [jnp.exp(a_cum[t]) for t in range(nt)]
    w_index = len(head_rows)
    head_rows += [jnp.exp(a_cum[nt - 1] - a_cum[t]) * dt[t] for t in range(nt)]
    cd_index = len(head_rows)
    head_rows.append(jnp.exp(a_cum[nt - 1]))
    chan = _expand_heads(jnp.concatenate(head_rows, axis=0), expand_ref[...])
    chan_rows = lambda i: chan[i * bb:(i + 1) * bb, :]

    y_diag = []
    for l in range(nt):
        acc = chan_rows(pair_index[(l, 0)]) * xs[0]
        for s in range(1, l + 1):
            acc = acc + chan_rows(pair_index[(l, s)]) * xs[s]
        y_diag.append(acc)
    xs_all = jnp.concatenate(xs, axis=0)
    xw_all = jnp.concatenate([chan_rows(w_index + t) * xs[t] for t in range(nt)], axis=0)
    ea_all = jnp.concatenate([chan_rows(ea_index + t) for t in range(nt)], axis=0)
    b_all = jnp.concatenate(bm, axis=0).astype(BF16)
    c_all = jnp.concatenate(cm, axis=0).astype(BF16)
    cd3 = jnp.concatenate([p.astype(F32) for p in _split3(chan_rows(cd_index))], axis=0)
    ones3 = jnp.ones((3 * bb, D_STATE), BF16)
    row_batch = lax.broadcasted_iota(jnp.int32, (rows, 1), 0) & (bb - 1)
    row_batch3 = lax.broadcasted_iota(jnp.int32, (3 * bb, 1), 0) & (bb - 1)
    contract_rows = (((0,), (0,)), ((), ()))

    def per_batch(b, y_off):
        st = state_ref[b]
        st16 = st.astype(BF16)
        mine = row_batch == b
        xw_b = jnp.where(mine, xw_all, 0.0).astype(BF16)
        cd_b = jnp.where(row_batch3 == b, cd3, 0.0).astype(BF16)
        scale = lax.dot_general(cd_b, ones3, contract_rows, preferred_element_type=F32)
        yo = []
        for g in range(SSM_GROUPS):
            grow = slice(g * GROUP_WIDTH, (g + 1) * GROUP_WIDTH)
            gst = slice(g * D_STATE, (g + 1) * D_STATE)
            yo.append(lax.dot_general(c_all[:, gst], st16[grow, :], (((1,), (1,)), ((), ())),
                                      preferred_element_type=F32))
            upd = lax.dot_general(xw_b[:, grow], b_all[:, gst], contract_rows,
                                  preferred_element_type=F32)
            nstate_ref[b, grow, :] = st[grow, :] * scale[grow, :] + upd
        return jnp.where(mine, jnp.concatenate(yo, axis=1), y_off)

    y_off = lax.fori_loop(0, bb, per_batch, jnp.zeros((rows, D_SSM), F32), unroll=True)
    y = jnp.concatenate(y_diag, axis=0) + y_off * ea_all + dskip_ref[...] * xs_all
    y = y * _silu(jnp.concatenate([at_time(z_ref, t) for t in range(nt)], axis=0))
    y_ssm = _rms(y, gssm_ref[...])
    for t in range(nt):
        mix_ref[:, t, 0:D_POOL] = y_pool[t * bb:(t + 1) * bb, :].astype(mix_ref.dtype)
        mix_ref[:, t, D_POOL:] = y_ssm[t * bb:(t + 1) * bb, :].astype(mix_ref.dtype)


def _sample_group_mix(u, z, xbc, dt, pbuf, cbuf, state, consts):
    nt, bb = SAMPLE_T, SAMPLE_BB
    batch = state.shape[0]
    tblk = lambda i: (0, i, 0)
    (wpool, pscale, convw, convb, dtb, alog, dskip_x, gssm, _, expand) = consts
    blk3 = lambda i: (i, 0, 0)
    fixed2 = lambda i: (0, 0)
    fixed3 = lambda i: (0, 0, 0)
    return pl.pallas_call(
        _sample_mix_kernel,
        grid=(batch // bb,),
        in_specs=[
            pl.BlockSpec((bb, nt, D_POOL), blk3),
            pl.BlockSpec((bb, nt, D_SSM), blk3),
            pl.BlockSpec((bb, nt, CONV_DIM), blk3),
            pl.BlockSpec((bb, nt, DT_PAD), blk3),
            pl.BlockSpec((POOL_BUF, bb, D_POOL), tblk),
            pl.BlockSpec((CONV_WIDTH - 1, bb, CONV_DIM), tblk),
            pl.BlockSpec((bb, D_SSM, D_STATE), blk3),
            pl.BlockSpec((len(POOL_WINDOWS), POOL_GROUP, POOL_GROUP), fixed3),
            pl.BlockSpec((1, D_POOL), fixed2),
            pl.BlockSpec((CONV_WIDTH, CONV_DIM), fixed2),
            pl.BlockSpec((1, CONV_DIM), fixed2),
            pl.BlockSpec((1, DT_PAD), fixed2),
            pl.BlockSpec((1, DT_PAD), fixed2),
            pl.BlockSpec((1, D_SSM), fixed2),
            pl.BlockSpec((1, D_SSM), fixed2),
            pl.BlockSpec((DT_PAD, D_SSM), fixed2),
        ],
        out_specs=[
            pl.BlockSpec((bb, nt, 2 * D_MODEL), blk3),
            pl.BlockSpec((POOL_BUF, bb, D_POOL), tblk),
            pl.BlockSpec((CONV_WIDTH - 1, bb, CONV_DIM), tblk),
            pl.BlockSpec((bb, D_SSM, D_STATE), blk3),
        ],
        out_shape=[
            jax.ShapeDtypeStruct((batch, nt, 2 * D_MODEL), F32),
            jax.ShapeDtypeStruct((POOL_BUF, batch, D_POOL), F32),
            jax.ShapeDtypeStruct((CONV_WIDTH - 1, batch, CONV_DIM), F32),
            jax.ShapeDtypeStruct((batch, D_SSM, D_STATE), F32),
        ],
        compiler_params=pltpu.CompilerParams(
            dimension_semantics=("arbitrary",), vmem_limit_bytes=VMEM_LIMIT),
        name="sample_mix",
    )(u, z, xbc, dt, pbuf, cbuf, state, wpool, pscale, convw, convb, dtb, alog, dskip_x, gssm, expand)


def _mix_constants(w_pool, pool_scale, conv_w, conv_b, dt_bias, a_log, d_skip, ssm_norm_g):
    pad = DT_PAD - SSM_HEADS
    head_of_channel = jnp.arange(D_SSM, dtype=jnp.int32) // HEAD_DIM
    slot_row = jnp.arange(DT_PAD, dtype=jnp.int32)
    expand = ((slot_row[:, None] % HEAD_SLOT == head_of_channel[None, :])
              & (slot_row[:, None] < 3 * HEAD_SLOT)).astype(BF16)
    tri = (jnp.arange(CHUNK)[:, None] >= jnp.arange(CHUNK)[None, :]).astype(BF16)
    return (
        w_pool.astype(BF16),
        pool_scale.reshape(1, D_POOL),
        conv_w,
        conv_b.reshape(1, CONV_DIM),
        jnp.pad(dt_bias, (0, pad)).reshape(1, DT_PAD),
        jnp.pad(a_log, (0, pad)).reshape(1, DT_PAD),
        jnp.repeat(d_skip, HEAD_DIM).reshape(1, D_SSM),
        ssm_norm_g.reshape(1, D_SSM),
        tri,
        expand,
    )


PROMPT_IN_TM = 512
SAMPLE_TB = 64


def kernel(x_prompt, x_sample, p_prompt, p_sample, state_pool, state_conv, state_ssm, w_in, w_pool, pool_scale, conv_w, conv_b, dt_bias, a_log, d_skip, ssm_norm_g, w_out, norm_mix_g, norm_mlp_g, w_ff1, w_ff2, norm_ple_g, w_gate, w_ple, final_norm_g):
    assert w_in.shape[0] == 1, "single-layer trunk"
    bp, tp = x_prompt.shape[:2]
    bs, ts = x_sample.shape[:2]
    assert tp % PROMPT_TILE == 0 and tp >= POOL_BUF and ts == SAMPLE_T and bs % SAMPLE_BB == 0

    row = lambda v: v.reshape(1, -1)
    w_in_t = w_in[0].T
    consts = _mix_constants(w_pool[0], pool_scale[0], conv_w[0], conv_b[0], dt_bias[0], a_log[0],
                            d_skip[0], ssm_norm_g[0])
    ffn_weights = (w_out[0].astype(BF16), row(norm_mlp_g[0]), w_ff1[0].astype(BF16),
                   w_ff2[0].astype(BF16), row(norm_ple_g[0]), w_gate[0].astype(BF16),
                   w_ple[0].astype(BF16), row(final_norm_g))
    g_mix = row(norm_mix_g[0])

    xp = x_prompt.reshape(bp * tp, D_MODEL)
    u, z, xbc, dt, w_in_t16 = _in_proj(xp, g_mix, w_in_t, PROMPT_IN_TM)
    y_prompt, ssm_p = _prompt_mix_ffn(u, z, xbc, dt, xp, p_prompt[0].reshape(bp * tp, D_PLE), consts,
                                      ffn_weights, bp, tp)
    pool_p = u.reshape(bp, tp, D_POOL)[:, tp - POOL_BUF:, :]
    conv_p = xbc.reshape(bp, tp, CONV_DIM)[:, tp - (CONV_WIDTH - 1):, :]

    tmaj = lambda a: jnp.transpose(a, (1, 0, 2))
    u_s, z_s, xbc_s, dt_s = _in_proj(x_sample, g_mix, w_in_t16, SAMPLE_TB)
    mix_s, pool_s, conv_s, ssm_s = _sample_group_mix(
        u_s, z_s, xbc_s, dt_s, tmaj(state_pool[0]), tmaj(state_conv[0]),
        state_ssm[0].reshape(bs, D_SSM, D_STATE), consts)
    y_sample = _out_ffn(x_sample, mix_s, p_sample[0], *ffn_weights, SAMPLE_TB)

    state_shape = (1, -1, SSM_HEADS, HEAD_DIM, D_STATE)
    return (y_prompt.reshape(bp, tp, D_MODEL), y_sample,
            pool_p[None], conv_p[None], ssm_p.reshape(state_shape),
            tmaj(pool_s)[None], tmaj(conv_s)[None], ssm_s.reshape(state_shape))
```

```python
import functools
import math

import jax
import jax.numpy as jnp
from jax import lax
from jax.experimental import pallas as pl
from jax.experimental.pallas import tpu as pltpu

F32 = jnp.float32
BF16 = jnp.bfloat16

D_MODEL = 1024
D_POOL = 1024
D_SSM = 1024
POOL_WINDOWS = (2, 4, 8, 16)
POOL_GROUP = 256
POOL_BUF = 15
SSM_HEADS = 16
HEAD_DIM = 64
SSM_GROUPS = 2
HEADS_PER_GROUP = 8
GROUP_WIDTH = HEADS_PER_GROUP * HEAD_DIM
D_STATE = 128
CONV_WIDTH = 4
CONV_DIM = D_SSM + 2 * SSM_GROUPS * D_STATE
CHUNK = 128
D_FF = 4096
D_PLE = 256
PAST_LEN = 16384
EPS = 1e-6
LOG2_E = 1.4426950408889634

LANES = 128
SUBLANES = 8
DT_PAD = LANES
HEAD_SLOT = SSM_HEADS
D_IN_PAD = D_POOL + D_SSM + CONV_DIM + DT_PAD
POOL_HIST = 16
CONV_HIST = SUBLANES
VMEM_LIMIT = 56 * 1024 * 1024
FFN_STAGES = 4


def _rms(x, g):
    return x * lax.rsqrt(jnp.mean(x * x, axis=-1, keepdims=True) + EPS) * g


def _silu(x):
    return x * jax.nn.sigmoid(x)


def _softplus(x):
    return jnp.maximum(x, 0.0) + jnp.log1p(jnp.exp(-jnp.abs(x)))


def _neg_exp_heads(a_log_row):
    lane = lax.broadcasted_iota(jnp.int32, a_log_row.shape, a_log_row.ndim - 1)
    return jnp.where(lane < SSM_HEADS, -jnp.exp(a_log_row), 0.0)


def _load_rows(ref):
    if len(ref.shape) == 2:
        return ref[...]
    return jnp.concatenate([ref[:, t, :] for t in range(ref.shape[1])], axis=0)


def _store_rows(ref, val):
    if len(ref.shape) == 2:
        ref[...] = val.astype(ref.dtype)
    else:
        nb = ref.shape[0]
        for t in range(ref.shape[1]):
            ref[:, t, :] = val[t * nb:(t + 1) * nb, :].astype(ref.dtype)


def _dot(a, b):
    return jnp.dot(a.astype(BF16), b.astype(BF16), preferred_element_type=F32)


def _split3(x):
    hi = x.astype(BF16)
    r1 = x - hi.astype(F32)
    mid = r1.astype(BF16)
    lo = (r1 - mid.astype(F32)).astype(BF16)
    return hi, mid, lo


def _pack_heads(v):
    lane = lax.broadcasted_iota(jnp.int32, v.shape, 1)
    hi, mid, lo = _split3(jnp.where(lane < SSM_HEADS, v, 0.0))
    packed = (hi.astype(F32) + pltpu.roll(mid.astype(F32), HEAD_SLOT, axis=1)
              + pltpu.roll(lo.astype(F32), 2 * HEAD_SLOT, axis=1))
    return packed.astype(BF16)


def _expand_heads(v, expand3):
    return jnp.dot(_pack_heads(v), expand3, preferred_element_type=F32)


def _chunk_cumsum(tri16, a):
    s = jnp.dot(tri16, _pack_heads(a), preferred_element_type=F32)
    return s + pltpu.roll(s, DT_PAD - HEAD_SLOT, axis=1) + pltpu.roll(s, DT_PAD - 2 * HEAD_SLOT, axis=1)


W_IN_CAST_ROWS = 240


def _in_proj_kernel(x_ref, g_ref, w_ref, u_ref, z_ref, xbc_ref, dt_ref, *wt16_out, n_in):
    if wt16_out:
        (wt_ref,) = wt16_out

        @pl.when(pl.program_id(0) == 0)
        def _():
            def cast_rows(k, carry):
                rows = pl.ds(pl.multiple_of(k * W_IN_CAST_ROWS, W_IN_CAST_ROWS), W_IN_CAST_ROWS)
                wt_ref[rows, :] = w_ref[rows, :].astype(BF16)
                return carry
            lax.fori_loop(0, n_in // W_IN_CAST_ROWS, cast_rows, 0)
            wt_ref[n_in:, :] = jnp.zeros((D_IN_PAD - n_in, D_MODEL), BF16)
    else:
        wt_ref = w_ref
    xn = _rms(_load_rows(x_ref), g_ref[...]).astype(BF16)
    start = 0
    for out_ref, width in ((u_ref, D_POOL), (z_ref, D_SSM), (xbc_ref, CONV_DIM), (dt_ref, DT_PAD)):
        _store_rows(out_ref, lax.dot_general(xn, wt_ref[start:start + width, :], (((1,), (1,)), ((), ())),
                                             preferred_element_type=F32))
        start += width


def _row_spec(a_shape, tm, width):
    if len(a_shape) == 2:
        return pl.BlockSpec((tm, width), lambda i: (i, 0))
    return pl.BlockSpec((tm, a_shape[1], width), lambda i: (i, 0, 0))


def _in_proj(x, g, w_in_t, tm):
    lead = x.shape[:-1]
    fixed = lambda i: (0, 0)
    widths = (D_POOL, D_SSM, CONV_DIM, DT_PAD)
    cast_here = w_in_t.dtype != BF16
    n_in = w_in_t.shape[0]
    assert (not cast_here and n_in == D_IN_PAD) or (n_in % W_IN_CAST_ROWS == 0 and n_in % (2 * SUBLANES) == 0)
    out_specs = [_row_spec(x.shape, tm, w) for w in widths]
    out_shape = [jax.ShapeDtypeStruct(lead + (w,), F32) for w in widths]
    if cast_here:
        out_specs.append(pl.BlockSpec((D_IN_PAD, D_MODEL), fixed))
        out_shape.append(jax.ShapeDtypeStruct((D_IN_PAD, D_MODEL), BF16))
    return pl.pallas_call(
        functools.partial(_in_proj_kernel, n_in=n_in),
        grid=(lead[0] // tm,),
        in_specs=[
            _row_spec(x.shape, tm, D_MODEL),
            pl.BlockSpec((1, D_MODEL), fixed),
            pl.BlockSpec(w_in_t.shape, fixed, pipeline_mode=pl.Buffered(1)),
        ],
        out_specs=out_specs,
        out_shape=out_shape,
        compiler_params=pltpu.CompilerParams(
            dimension_semantics=("arbitrary",), vmem_limit_bytes=VMEM_LIMIT),
        name="in_proj",
    )(x, g, w_in_t)


def _run(steps):
    try:
        while True:
            next(steps)
    except StopIteration as done:
        return done.value


def _interleave(main, side, side_per_main):
    result = None
    main_done = side_done = False
    while not (main_done and side_done):
        if not main_done:
            try:
                next(main)
            except StopIteration as done:
                result, main_done = done.value, True
        for _ in range(side_per_main):
            if side_done:
                break
            try:
                next(side)
            except StopIteration:
                side_done = True
    return result


def _ffn_tile_steps(x, mix16, p, wout_ref, g_mlp_ref, w1_ref, w2_ref, g_ple_ref, wg_ref, wple_ref, g_fin_ref):
    h = x + jnp.dot(mix16, wout_ref[...], preferred_element_type=F32)
    yield
    hn = _rms(h, g_mlp_ref[...]).astype(BF16)
    ff_block = D_FF // FFN_STAGES
    for c in range(FFN_STAGES):
        f = jnp.dot(hn, w1_ref[:, c * ff_block:(c + 1) * ff_block], preferred_element_type=F32)
        f = jnp.square(jnp.maximum(f, 0.0)).astype(BF16)
        h = h + jnp.dot(f, w2_ref[c * ff_block:(c + 1) * ff_block, :], preferred_element_type=F32)
        yield
    gate = jax.nn.sigmoid(
        jnp.dot(_rms(h, g_ple_ref[...]).astype(BF16), wg_ref[...], preferred_element_type=F32))
    h = h + gate * jnp.dot(p.astype(BF16), wple_ref[...], preferred_element_type=F32)
    return _rms(h, g_fin_ref[...])


def _out_ffn_kernel(x_ref, mix_ref, p_ref, *refs):
    *weight_refs, y_ref = refs
    _store_rows(y_ref, _run(_ffn_tile_steps(_load_rows(x_ref), _load_rows(mix_ref).astype(BF16),
                                            _load_rows(p_ref), *weight_refs)))


def _out_ffn(x, mix, p, wout, g_mlp, w1, w2, g_ple, wg, wple, g_fin, tm):
    fixed = lambda i: (0, 0)

    def resident(shape):
        return pl.BlockSpec(shape, fixed, pipeline_mode=pl.Buffered(1))

    return pl.pallas_call(
        _out_ffn_kernel,
        grid=(x.shape[0] // tm,),
        in_specs=[
            _row_spec(x.shape, tm, D_MODEL),
            _row_spec(x.shape, tm, 2 * D_MODEL),
            _row_spec(x.shape, tm, D_PLE),
            resident((2 * D_MODEL, D_MODEL)),
            resident((1, D_MODEL)),
            resident((D_MODEL, D_FF)),
            resident((D_FF, D_MODEL)),
            resident((1, D_MODEL)),
            resident((D_MODEL, D_MODEL)),
            resident((D_PLE, D_MODEL)),
            resident((1, D_MODEL)),
        ],
        out_specs=_row_spec(x.shape, tm, D_MODEL),
        out_shape=jax.ShapeDtypeStruct(x.shape, F32),
        compiler_params=pltpu.CompilerParams(
            dimension_semantics=("arbitrary",), vmem_limit_bytes=VMEM_LIMIT),
        name="out_ffn",
    )(x, mix, p, wout, g_mlp, w1, w2, g_ple, wg, wple, g_fin)


PROMPT_TILE = 256


def _mix_chunk_steps(u, z, xbc, dt_raw, pos0, keep, const_refs, pool_hist, conv_hist, state_t):
    (wpool_ref, pscale_ref, convw_ref, convb_ref, dtb_ref, alog_ref, dskip_ref, gssm_ref, tri_ref,
     expand_ref) = const_refs
    carried = (lambda v: v) if keep is None else (lambda v: v * keep)

    ext = jnp.concatenate([carried(pool_hist[...]), u], axis=0)
    pool_hist[...] = u[CHUNK - POOL_HIST:, :]
    pos = pos0 + lax.broadcasted_iota(jnp.int32, (CHUNK, 1), 0)
    y_pool = []
    for g, w in enumerate(POOL_WINDOWS):
        cols = slice(g * POOL_GROUP, (g + 1) * POOL_GROUP)
        s = ext[:, cols]
        shift = 1
        while shift < w:
            s = s + pltpu.roll(s, shift, axis=0)
            shift *= 2
        count = jnp.minimum(w, pos + 1).astype(F32)
        d = s[POOL_HIST:, :] / count - u[:, cols]
        y_pool.append(_dot(d, wpool_ref[g]))
    y_pool = jnp.concatenate(y_pool, axis=1) * pscale_ref[...]
    yield

    extc = jnp.concatenate([carried(conv_hist[...]), xbc], axis=0)
    conv_hist[...] = xbc[CHUNK - CONV_HIST:, :]
    acc = convb_ref[...]
    for k in range(CONV_WIDTH):
        back = CONV_WIDTH - 1 - k
        tap = extc if back == 0 else pltpu.roll(extc, back, axis=0)
        acc = acc + tap[CONV_HIST:, :] * convw_ref[k:k + 1, :]
    xc = _silu(acc)
    xs = xc[:, 0:D_SSM]
    b_all = xc[:, D_SSM:D_SSM + SSM_GROUPS * D_STATE]
    c_all = xc[:, D_SSM + SSM_GROUPS * D_STATE:CONV_DIM]
    yield

    dt = _softplus(dt_raw + dtb_ref[...])
    a_cum = _chunk_cumsum(tri_ref[...], dt * _neg_exp_heads(alog_ref[...]))
    a_last = a_cum[CHUNK - 1:CHUNK, :]
    expand = expand_ref[...]
    dt_t = dt.T
    a_log2 = a_cum * LOG2_E
    a_log2_t = a_log2.T
    ea_x = _expand_heads(jnp.exp(a_cum), expand)
    cd_x = _expand_heads(jnp.broadcast_to(jnp.exp(a_last), (2 * SUBLANES, DT_PAD)), expand)[0:1, :]
    x_w = xs * _expand_heads(jnp.exp(a_last - a_cum) * dt, expand)
    yield

    row_i = lax.broadcasted_iota(jnp.int32, (CHUNK, CHUNK), 0)
    col_i = lax.broadcasted_iota(jnp.int32, (CHUNK, CHUNK), 1)
    causal = row_i >= col_i
    lane = lax.broadcasted_iota(jnp.int32, (CHUNK, 2 * HEAD_DIM), 1)
    first_head = lane < HEAD_DIM

    y_parts = []
    for g in range(SSM_GROUPS):
        gch = slice(g * GROUP_WIDTH, (g + 1) * GROUP_WIDTH)
        b_g = b_all[:, g * D_STATE:(g + 1) * D_STATE].astype(BF16)
        c_g = c_all[:, g * D_STATE:(g + 1) * D_STATE].astype(BF16)
        cb = lax.dot_general(c_g, b_g, (((1,), (1,)), ((), ())), preferred_element_type=F32).astype(BF16)
        st_g = carried(state_t[:, gch])
        y_off = jnp.dot(c_g, st_g.astype(BF16), preferred_element_type=F32) * ea_x[:, gch]
        y_diag = []
        for j in range(HEADS_PER_GROUP // 2):
            h0 = g * HEADS_PER_GROUP + 2 * j
            m = []
            for h in (h0, h0 + 1):
                seg = a_log2[:, h:h + 1] - a_log2_t[h:h + 1, :]
                decay = jnp.exp2(jnp.where(causal, seg, -jnp.inf)).astype(BF16)
                dt_cols = jnp.broadcast_to(dt_t[h:h + 1, :], (CHUNK, CHUNK)).astype(BF16)
                m.append(decay * cb * dt_cols)
            pair = xs[:, h0 * HEAD_DIM:(h0 + 2) * HEAD_DIM]
            rhs = jnp.concatenate([jnp.where(first_head, pair, 0.0),
                                   jnp.where(first_head, 0.0, pair)], axis=0).astype(BF16)
            y_diag.append(jnp.dot(jnp.concatenate(m, axis=1), rhs, preferred_element_type=F32))
            yield
        y_parts.append(jnp.concatenate(y_diag, axis=1) + y_off)
        upd = lax.dot_general(b_g, x_w[:, gch].astype(BF16), (((0,), (0,)), ((), ())),
                              preferred_element_type=F32)
        state_t[:, gch] = st_g * cd_x[:, gch] + upd

    y = jnp.concatenate(y_parts, axis=1) + dskip_ref[...] * xs
    y = y * _silu(z)
    return y_pool, _rms(y, gssm_ref[...])


N_MIX_CONSTS = 10
N_FFN_WEIGHTS = 8
MIXER_STAGES_PER_FFN_STAGE = 4


def _prompt_mix_ffn_kernel(u_ref, z_ref, xbc_ref, dt_ref, x_ref, p_ref, *refs, n_tiles, tiles_per_seq):
    const_refs = refs[:N_MIX_CONSTS]
    weight_refs = refs[N_MIX_CONSTS:N_MIX_CONSTS + N_FFN_WEIGHTS]
    y_ref, ssm_ref, mix_scr, pool_hist, conv_hist, state_t = refs[N_MIX_CONSTS + N_FFN_WEIGHTS:]
    s = pl.program_id(0)

    @pl.when(s == 0)
    def _():
        mix_scr[...] = jnp.zeros_like(mix_scr)
        pool_hist[...] = jnp.zeros_like(pool_hist)
        conv_hist[...] = jnp.zeros_like(conv_hist)
        state_t[...] = jnp.zeros_like(state_t)

    tile_in_seq = jnp.minimum(s, n_tiles - 1) % tiles_per_seq
    keep = jnp.where(tile_in_seq == 0, 0.0, 1.0)
    slot = s % 2

    def mixer_steps():
        for ci in range(PROMPT_TILE // CHUNK):
            rows = slice(ci * CHUNK, (ci + 1) * CHUNK)
            y_pool, y_ssm = yield from _mix_chunk_steps(
                u_ref[rows, :], z_ref[rows, :], xbc_ref[rows, :], dt_ref[rows, :],
                tile_in_seq * PROMPT_TILE + ci * CHUNK, keep if ci == 0 else None,
                const_refs, pool_hist, conv_hist, state_t)
            mix_scr[slot, rows, 0:D_POOL] = y_pool.astype(BF16)
            mix_scr[slot, rows, D_POOL:] = y_ssm.astype(BF16)
            yield

    ffn_steps = _ffn_tile_steps(x_ref[...], mix_scr[(s + 1) % 2], p_ref[...], *weight_refs)
    y_ref[...] = _interleave(ffn_steps, mixer_steps(), MIXER_STAGES_PER_FFN_STAGE)

    @pl.when((tile_in_seq == tiles_per_seq - 1) & (s < n_tiles))
    def _():
        for j in range(D_SSM // LANES):
            ssm_ref[j * LANES:(j + 1) * LANES, :] = state_t[:, j * LANES:(j + 1) * LANES].T


def _prompt_mix_ffn(u, z, xbc, dt, x2d, p2d, consts, ffn_weights, batch, seq):
    tiles_per_seq = seq // PROMPT_TILE
    n_tiles = batch * tiles_per_seq
    mix_row = lambda s: (jnp.minimum(s, n_tiles - 1), 0)
    ffn_row = lambda s: (jnp.maximum(s - 1, 0), 0)
    fixed2 = lambda s: (0, 0)
    fixed3 = lambda s: (0, 0, 0)

    def resident(a):
        return pl.BlockSpec(a.shape, fixed3 if a.ndim == 3 else fixed2, pipeline_mode=pl.Buffered(1))

    return pl.pallas_call(
        functools.partial(_prompt_mix_ffn_kernel, n_tiles=n_tiles, tiles_per_seq=tiles_per_seq),
        grid=(n_tiles + 1,),
        in_specs=[
            pl.BlockSpec((PROMPT_TILE, D_POOL), mix_row),
            pl.BlockSpec((PROMPT_TILE, D_SSM), mix_row),
            pl.BlockSpec((PROMPT_TILE, CONV_DIM), mix_row),
            pl.BlockSpec((PROMPT_TILE, DT_PAD), mix_row),
            pl.BlockSpec((PROMPT_TILE, D_MODEL), ffn_row),
            pl.BlockSpec((PROMPT_TILE, D_PLE), ffn_row),
        ] + [resident(a) for a in consts] + [resident(a) for a in ffn_weights],
        out_specs=[
            pl.BlockSpec((PROMPT_TILE, D_MODEL), ffn_row),
            pl.BlockSpec((None, D_SSM, D_STATE),
                         lambda s: (jnp.minimum(s, n_tiles - 1) // tiles_per_seq, 0, 0)),
        ],
        out_shape=[
            jax.ShapeDtypeStruct((batch * seq, D_MODEL), F32),
            jax.ShapeDtypeStruct((batch, D_SSM, D_STATE), F32),
        ],
        scratch_shapes=[
            pltpu.VMEM((2, PROMPT_TILE, 2 * D_MODEL), BF16),
            pltpu.VMEM((POOL_HIST, D_POOL), F32),
            pltpu.VMEM((CONV_HIST, CONV_DIM), F32),
            pltpu.VMEM((D_STATE, D_SSM), F32),
        ],
        compiler_params=pltpu.CompilerParams(
            dimension_semantics=("arbitrary",), vmem_limit_bytes=VMEM_LIMIT),
        name="prompt_mix_ffn",
    )(u, z, xbc, dt, x2d, p2d, *consts, *ffn_weights)


SAMPLE_T = 4
SAMPLE_BB = 16


def _sample_mix_kernel(u_ref, z_ref, xbc_ref, dt_ref, pbuf_ref, cbuf_ref, state_ref,
                       wpool_ref, pscale_ref, convw_ref, convb_ref, dtb_ref, alog_ref, dskip_ref,
                       gssm_ref, expand_ref,
                       mix_ref, npool_ref, nconv_ref, nstate_ref):
    nt, bb = SAMPLE_T, SAMPLE_BB
    rows = nt * bb

    at_time = lambda ref, t: ref[:, t, :]
    ext = [pbuf_ref[i] for i in range(POOL_BUF)] + [at_time(u_ref, t) for t in range(nt)]
    for i in range(POOL_BUF):
        npool_ref[i] = ext[i + nt]
    y_pool = []
    for g, w in enumerate(POOL_WINDOWS):
        cols = slice(g * POOL_GROUP, (g + 1) * POOL_GROUP)
        d = []
        for t in range(nt):
            hi = POOL_BUF + t
            win = ext[hi - w + 1][:, cols]
            for i in range(hi - w + 2, hi + 1):
                win = win + ext[i][:, cols]
            count = float(min(w, PAST_LEN + t + 1))
            d.append(win / count - ext[hi][:, cols])
        y_pool.append(_dot(jnp.concatenate(d, axis=0), wpool_ref[g]))
    y_pool = jnp.concatenate(y_pool, axis=1) * pscale_ref[...]

    extc = [cbuf_ref[i] for i in range(CONV_WIDTH - 1)] + [at_time(xbc_ref, t) for t in range(nt)]
    for j in range(CONV_WIDTH - 1):
        nconv_ref[j] = extc[nt + j]
    xc = []
    for t in range(nt):
        acc = convb_ref[...]
        for k in range(CONV_WIDTH):
            acc = acc + extc[t + k] * convw_ref[k:k + 1, :]
        xc.append(_silu(acc))
    xs = [v[:, 0:D_SSM] for v in xc]
    bm = [v[:, D_SSM:D_SSM + SSM_GROUPS * D_STATE] for v in xc]
    cm = [v[:, D_SSM + SSM_GROUPS * D_STATE:CONV_DIM] for v in xc]

    a_row = _neg_exp_heads(alog_ref[...])
    dt = [_softplus(at_time(dt_ref, t) + dtb_ref[...]) for t in range(nt)]
    a_cum = []
    for t in range(nt):
        a_t = dt[t] * a_row
        a_cum.append(a_t if t == 0 else a_cum[-1] + a_t)
    lane = lax.broadcasted_iota(jnp.int32, (bb, DT_PAD), 1)
    in_group0 = lane < HEADS_PER_GROUP
    head_rows = []
    pair_index = {}
    for l in range(nt):
        for s in range(l + 1):
            cb = [jnp.sum(cm[l][:, g * D_STATE:(g + 1) * D_STATE] * bm[s][:, g * D_STATE:(g + 1) * D_STATE],
                          axis=-1, keepdims=True) for g in range(SSM_GROUPS)]
            cb_heads = jnp.where(in_group0, cb[0], cb[1])
            pair_index[(l, s)] = len(head_rows)
            head_rows.append(cb_heads * jnp.exp(a_cum[l] - a_cum[s]) * dt[s])
    ea_index = len(head_rows)
    head_rows += [jnp.exp(a_cum[t]) for t in range(nt)]
    w_index = len(head_rows)
    head_rows += [jnp.exp(a_cum[nt - 1] - a_cum[t]) * dt[t] for t in range(nt)]
    cd_index = len(head_rows)
    head_rows.append(jnp.exp(a_cum[nt - 1]))
    chan = _expand_heads(jnp.concatenate(head_rows, axis=0), expand_ref[...])
    chan_rows = lambda i: chan[i * bb:(i + 1) * bb, :]

    y_diag = []
    for l in range(nt):
        acc = chan_rows(pair_index[(l, 0)]) * xs[0]
        for s in range(1, l + 1):
            acc = acc + chan_rows(pair_index[(l, s)]) * xs[s]
        y_diag.append(acc)
    xs_all = jnp.concatenate(xs, axis=0)
    xw_all = jnp.concatenate([chan_rows(w_index + t) * xs[t] for t in range(nt)], axis=0)
    ea_all = jnp.concatenate([chan_rows(ea_index + t) for t in range(nt)], axis=0)
    b_all = jnp.concatenate(bm, axis=0).astype(BF16)
    c_all = jnp.concatenate(cm, axis=0).astype(BF16)
    cd3 = jnp.concatenate([p.astype(F32) for p in _split3(chan_rows(cd_index))], axis=0)
    ones3 = jnp.ones((3 * bb, D_STATE), BF16)
    row_batch = lax.broadcasted_iota(jnp.int32, (rows, 1), 0) & (bb - 1)
    row_batch3 = lax.broadcasted_iota(jnp.int32, (3 * bb, 1), 0) & (bb - 1)
    contract_rows = (((0,), (0,)), ((), ()))

    def per_batch(b, y_off):
        st = state_ref[b]
        st16 = st.astype(BF16)
        mine = row_batch == b
        xw_b = jnp.where(mine, xw_all, 0.0).astype(BF16)
        cd_b = jnp.where(row_batch3 == b, cd3, 0.0).astype(BF16)
        scale = lax.dot_general(cd_b, ones3, contract_rows, preferred_element_type=F32)
        yo = []
        for g in range(SSM_GROUPS):
            grow = slice(g * GROUP_WIDTH, (g + 1) * GROUP_WIDTH)
            gst = slice(g * D_STATE, (g + 1) * D_STATE)
            yo.append(lax.dot_general(c_all[:, gst], st16[grow, :], (((1,), (1,)), ((), ())),
                                      preferred_element_type=F32))
            upd = lax.dot_general(xw_b[:, grow], b_all[:, gst], contract_rows,
                                  preferred_element_type=F32)
            nstate_ref[b, grow, :] = st[grow, :] * scale[grow, :] + upd
        return jnp.where(mine, jnp.concatenate(yo, axis=1), y_off)

    y_off = lax.fori_loop(0, bb, per_batch, jnp.zeros((rows, D_SSM), F32), unroll=True)
    y = jnp.concatenate(y_diag, axis=0) + y_off * ea_all + dskip_ref[...] * xs_all
    y = y * _silu(jnp.concatenate([at_time(z_ref, t) for t in range(nt)], axis=0))
    y_ssm = _rms(y, gssm_ref[...])
    for t in range(nt):
        mix_ref[:, t, 0:D_POOL] = y_pool[t * bb:(t + 1) * bb, :].astype(mix_ref.dtype)
        mix_ref[:, t, D_POOL:] = y_ssm[t * bb:(t + 1) * bb, :].astype(mix_ref.dtype)


def _sample_group_mix(u, z, xbc, dt, pbuf, cbuf, state, consts):
    nt, bb = SAMPLE_T, SAMPLE_BB
    batch = state.shape[0]
    tblk = lambda i: (0, i, 0)
    (wpool, pscale, convw, convb, dtb, alog, dskip_x, gssm, _, expand) = consts
    blk3 = lambda i: (i, 0, 0)
    fixed2 = lambda i: (0, 0)
    fixed3 = lambda i: (0, 0, 0)
    return pl.pallas_call(
        _sample_mix_kernel,
        grid=(batch // bb,),
        in_specs=[
            pl.BlockSpec((bb, nt, D_POOL), blk3),
            pl.BlockSpec((bb, nt, D_SSM), blk3),
            pl.BlockSpec((bb, nt, CONV_DIM), blk3),
            pl.BlockSpec((bb, nt, DT_PAD), blk3),
            pl.BlockSpec((POOL_BUF, bb, D_POOL), tblk),
            pl.BlockSpec((CONV_WIDTH - 1, bb, CONV_DIM), tblk),
            pl.BlockSpec((bb, D_SSM, D_STATE), blk3),
            pl.BlockSpec((len(POOL_WINDOWS), POOL_GROUP, POOL_GROUP), fixed3),
            pl.BlockSpec((1, D_POOL), fixed2),
            pl.BlockSpec((CONV_WIDTH, CONV_DIM), fixed2),
            pl.BlockSpec((1, CONV_DIM), fixed2),
            pl.BlockSpec((1, DT_PAD), fixed2),
            pl.BlockSpec((1, DT_PAD), fixed2),
            pl.BlockSpec((1, D_SSM), fixed2),
            pl.BlockSpec((1, D_SSM), fixed2),
            pl.BlockSpec((DT_PAD, D_SSM), fixed2),
        ],
        out_specs=[
            pl.BlockSpec((bb, nt, 2 * D_MODEL), blk3),
            pl.BlockSpec((POOL_BUF, bb, D_POOL), tblk),
            pl.BlockSpec((CONV_WIDTH - 1, bb, CONV_DIM), tblk),
            pl.BlockSpec((bb, D_SSM, D_STATE), blk3),
        ],
        out_shape=[
            jax.ShapeDtypeStruct((batch, nt, 2 * D_MODEL), F32),
            jax.ShapeDtypeStruct((POOL_BUF, batch, D_POOL), F32),
            jax.ShapeDtypeStruct((CONV_WIDTH - 1, batch, CONV_DIM), F32),
            jax.ShapeDtypeStruct((batch, D_SSM, D_STATE), F32),
        ],
        compiler_params=pltpu.CompilerParams(
            dimension_semantics=("arbitrary",), vmem_limit_bytes=VMEM_LIMIT),
        name="sample_mix",
    )(u, z, xbc, dt, pbuf, cbuf, state, wpool, pscale, convw, convb, dtb, alog, dskip_x, gssm, expand)


def _mix_constants(w_pool, pool_scale, conv_w, conv_b, dt_bias, a_log, d_skip, ssm_norm_g):
    pad = DT_PAD - SSM_HEADS
    head_of_channel = jnp.arange(D_SSM, dtype=jnp.int32) // HEAD_DIM
    slot_row = jnp.arange(DT_PAD, dtype=jnp.int32)
    expand = ((slot_row[:, None] % HEAD_SLOT == head_of_channel[None, :])
              & (slot_row[:, None] < 3 * HEAD_SLOT)).astype(BF16)
    tri = (jnp.arange(CHUNK)[:, None] >= jnp.arange(CHUNK)[None, :]).astype(BF16)
    return (
        w_pool.astype(BF16),
        pool_scale.reshape(1, D_POOL),
        conv_w,
        conv_b.reshape(1, CONV_DIM),
        jnp.pad(dt_bias, (0, pad)).reshape(1, DT_PAD),
        jnp.pad(a_log, (0, pad)).reshape(1, DT_PAD),
        jnp.repeat(d_skip, HEAD_DIM).reshape(1, D_SSM),
        ssm_norm_g.reshape(1, D_SSM),
        tri,
        expand,
    )


PROMPT_IN_TM = 512
SAMPLE_TB = 64


def kernel(x_prompt, x_sample, p_prompt, p_sample, state_pool, state_conv, state_ssm, w_in, w_pool, pool_scale, conv_w, conv_b, dt_bias, a_log, d_skip, ssm_norm_g, w_out, norm_mix_g, norm_mlp_g, w_ff1, w_ff2, norm_ple_g, w_gate, w_ple, final_norm_g):
    assert w_in.shape[0] == 1, "single-layer trunk"
    bp, tp = x_prompt.shape[:2]
    bs, ts = x_sample.shape[:2]
    assert tp % PROMPT_TILE == 0 and tp >= POOL_BUF and ts == SAMPLE_T and bs % SAMPLE_BB == 0

    row = lambda v: v.reshape(1, -1)
    w_in_t = w_in[0].T
    consts = _mix_constants(w_pool[0], pool_scale[0], conv_w[0], conv_b[0], dt_bias[0], a_log[0],
                            d_skip[0], ssm_norm_g[0])
    ffn_weights = (w_out[0].astype(BF16), row(norm_mlp_g[0]), w_ff1[0].astype(BF16),
                   w_ff2[0].astype(BF16), row(norm_ple_g[0]), w_gate[0].astype(BF16),
                   w_ple[0].astype(BF16), row(final_norm_g))
    g_mix = row(norm_mix_g[0])

    xp = x_prompt.reshape(bp * tp, D_MODEL)
    u, z, xbc, dt, w_in_t16 = _in_proj(xp, g_mix, w_in_t, PROMPT_IN_TM)
    y_prompt, ssm_p = _prompt_mix_ffn(u, z, xbc, dt, xp, p_prompt[0].reshape(bp * tp, D_PLE), consts,
                                      ffn_weights, bp, tp)
    pool_p = u.reshape(bp, tp, D_POOL)[:, tp - POOL_BUF:, :]
    conv_p = xbc.reshape(bp, tp, CONV_DIM)[:, tp - (CONV_WIDTH - 1):, :]

    tmaj = lambda a: jnp.transpose(a, (1, 0, 2))
    u_s, z_s, xbc_s, dt_s = _in_proj(x_sample, g_mix, w_in_t16, SAMPLE_TB)
    mix_s, pool_s, conv_s, ssm_s = _sample_group_mix(
        u_s, z_s, xbc_s, dt_s, tmaj(state_pool[0]), tmaj(state_conv[0]),
        state_ssm[0].reshape(bs, D_SSM, D_STATE), consts)
    y_sample = _out_ffn(x_sample, mix_s, p_sample[0], *ffn_weights, SAMPLE_TB)

    state_shape = (1, -1, SSM_HEADS, HEAD_DIM, D_STATE)
    return (y_prompt.reshape(bp, tp, D_MODEL), y_sample,
            pool_p[None], conv_p[None], ssm_p.reshape(state_shape),
            tmaj(pool_s)[None], tmaj(conv_s)[None], ssm_s.reshape(state_shape))
```

```python
import functools
import math

import jax
import jax.numpy as jnp
from jax import lax
from jax.experimental import pallas as pl
from jax.experimental.pallas import tpu as pltpu

F32 = jnp.float32
BF16 = jnp.bfloat16

D_MODEL = 1024
D_POOL = 1024
D_SSM = 1024
POOL_WINDOWS = (2, 4, 8, 16)
POOL_GROUP = 256
POOL_BUF = 15
SSM_HEADS = 16
HEAD_DIM = 64
SSM_GROUPS = 2
HEADS_PER_GROUP = 8
GROUP_WIDTH = HEADS_PER_GROUP * HEAD_DIM
D_STATE = 128
CONV_WIDTH = 4
CONV_DIM = D_SSM + 2 * SSM_GROUPS * D_STATE
CHUNK = 128
D_FF = 4096
D_PLE = 256
PAST_LEN = 16384
EPS = 1e-6
LOG2_E = 1.4426950408889634

LANES = 128
SUBLANES = 8
DT_PAD = LANES
HEAD_SLOT = SSM_HEADS
D_IN_PAD = D_POOL + D_SSM + CONV_DIM + DT_PAD
POOL_HIST = 16
CONV_HIST = SUBLANES
VMEM_LIMIT = 56 * 1024 * 1024
FFN_STAGES = 4


def _rms(x, g):
    return x * lax.rsqrt(jnp.mean(x * x, axis=-1, keepdims=True) + EPS) * g


def _silu(x):
    return x * jax.nn.sigmoid(x)


def _softplus(x):
    return jnp.maximum(x, 0.0) + jnp.log1p(jnp.exp(-jnp.abs(x)))


def _neg_exp_heads(a_log_row):
    lane = lax.broadcasted_iota(jnp.int32, a_log_row.shape, a_log_row.ndim - 1)
    return jnp.where(lane < SSM_HEADS, -jnp.exp(a_log_row), 0.0)


def _load_rows(ref):
    if len(ref.shape) == 2:
        return ref[...]
    return jnp.concatenate([ref[:, t, :] for t in range(ref.shape[1])], axis=0)


def _store_rows(ref, val):
    if len(ref.shape) == 2:
        ref[...] = val.astype(ref.dtype)
    else:
        nb = ref.shape[0]
        for t in range(ref.shape[1]):
            ref[:, t, :] = val[t * nb:(t + 1) * nb, :].astype(ref.dtype)


def _dot(a, b):
    return jnp.dot(a.astype(BF16), b.astype(BF16), preferred_element_type=F32)


def _split3(x):
    hi = x.astype(BF16)
    r1 = x - hi.astype(F32)
    mid = r1.astype(BF16)
    lo = (r1 - mid.astype(F32)).astype(BF16)
    return hi, mid, lo


def _pack_heads(v):
    lane = lax.broadcasted_iota(jnp.int32, v.shape, 1)
    hi, mid, lo = _split3(jnp.where(lane < SSM_HEADS, v, 0.0))
    packed = (hi.astype(F32) + pltpu.roll(mid.astype(F32), HEAD_SLOT, axis=1)
              + pltpu.roll(lo.astype(F32), 2 * HEAD_SLOT, axis=1))
    return packed.astype(BF16)


def _expand_heads(v, expand3):
    return jnp.dot(_pack_heads(v), expand3, preferred_element_type=F32)


def _chunk_cumsum(tri16, a):
    s = jnp.dot(tri16, _pack_heads(a), preferred_element_type=F32)
    return s + pltpu.roll(s, DT_PAD - HEAD_SLOT, axis=1) + pltpu.roll(s, DT_PAD - 2 * HEAD_SLOT, axis=1)


W_IN_CAST_ROWS = 240


def _in_proj_kernel(x_ref, g_ref, w_ref, *refs, n_in, n_ride):
    ride_in, (u_ref, z_ref, xbc_ref, dt_ref), rest = refs[:n_ride], refs[n_ride:n_ride + 4], refs[n_ride + 4:]
    wt16_out, ride_out = rest[:len(rest) - n_ride], rest[len(rest) - n_ride:]
    for src, dst in zip(ride_in, ride_out):
        dst[...] = src[...].astype(BF16)
    if wt16_out:
        (wt_ref,) = wt16_out

        @pl.when(pl.program_id(0) == 0)
        def _():
            def cast_rows(k, carry):
                rows = pl.ds(pl.multiple_of(k * W_IN_CAST_ROWS, W_IN_CAST_ROWS), W_IN_CAST_ROWS)
                wt_ref[rows, :] = w_ref[rows, :].astype(BF16)
                return carry
            lax.fori_loop(0, n_in // W_IN_CAST_ROWS, cast_rows, 0)
            wt_ref[n_in:, :] = jnp.zeros((D_IN_PAD - n_in, D_MODEL), BF16)
    else:
        wt_ref = w_ref
    xn = _rms(_load_rows(x_ref), g_ref[...]).astype(BF16)
    start = 0
    for out_ref, width in ((u_ref, D_POOL), (z_ref, D_SSM), (xbc_ref, CONV_DIM), (dt_ref, DT_PAD)):
        _store_rows(out_ref, lax.dot_general(xn, wt_ref[start:start + width, :], (((1,), (1,)), ((), ())),
                                             preferred_element_type=F32))
        start += width


def _row_spec(a_shape, tm, width):
    if len(a_shape) == 2:
        return pl.BlockSpec((tm, width), lambda i: (i, 0))
    return pl.BlockSpec((tm, a_shape[1], width), lambda i: (i, 0, 0))


def _in_proj(x, g, w_in_t, tm, ride_along=()):
    lead = x.shape[:-1]
    n_steps = lead[0] // tm
    fixed = lambda i: (0, 0)
    widths = (D_POOL, D_SSM, CONV_DIM, DT_PAD)
    cast_here = w_in_t.dtype != BF16
    n_in = w_in_t.shape[0]
    assert (not cast_here and n_in == D_IN_PAD) or (n_in % W_IN_CAST_ROWS == 0 and n_in % (2 * SUBLANES) == 0)
    out_specs = [_row_spec(x.shape, tm, w) for w in widths]
    out_shape = [jax.ShapeDtypeStruct(lead + (w,), F32) for w in widths]
    if cast_here:
        out_specs.append(pl.BlockSpec((D_IN_PAD, D_MODEL), fixed))
        out_shape.append(jax.ShapeDtypeStruct((D_IN_PAD, D_MODEL), BF16))
    ride_specs = []
    for w in ride_along:
        slab = w.shape[0] // n_steps
        assert slab * n_steps == w.shape[0] and slab % (2 * SUBLANES) == 0
        ride_specs.append(pl.BlockSpec((slab, w.shape[1]), lambda i: (i, 0)))
        out_shape.append(jax.ShapeDtypeStruct(w.shape, BF16))
    return pl.pallas_call(
        functools.partial(_in_proj_kernel, n_in=n_in, n_ride=len(ride_along)),
        grid=(n_steps,),
        in_specs=[
            _row_spec(x.shape, tm, D_MODEL),
            pl.BlockSpec((1, D_MODEL), fixed),
            pl.BlockSpec(w_in_t.shape, fixed, pipeline_mode=pl.Buffered(1)),
        ] + ride_specs,
        out_specs=out_specs + ride_specs,
        out_shape=out_shape,
        compiler_params=pltpu.CompilerParams(
            dimension_semantics=("arbitrary",), vmem_limit_bytes=VMEM_LIMIT),
        name="in_proj",
    )(x, g, w_in_t, *ride_along)


def _run(steps):
    try:
        while True:
            next(steps)
    except StopIteration as done:
        return done.value


def _interleave(main, side, side_per_main):
    result = None
    main_done = side_done = False
    while not (main_done and side_done):
        if not main_done:
            try:
                next(main)
            except StopIteration as done:
                result, main_done = done.value, True
        for _ in range(side_per_main):
            if side_done:
                break
            try:
                next(side)
            except StopIteration:
                side_done = True
    return result


def _ffn_tile_steps(x, mix16, p, wout_ref, g_mlp_ref, w1_ref, w2_ref, g_ple_ref, wg_ref, wple_ref, g_fin_ref):
    h = x + jnp.dot(mix16, wout_ref[...], preferred_element_type=F32)
    yield
    hn = _rms(h, g_mlp_ref[...]).astype(BF16)
    ff_block = D_FF // FFN_STAGES
    for c in range(FFN_STAGES):
        f = jnp.dot(hn, w1_ref[:, c * ff_block:(c + 1) * ff_block], preferred_element_type=F32)
        f = jnp.square(jnp.maximum(f, 0.0)).astype(BF16)
        h = h + jnp.dot(f, w2_ref[c * ff_block:(c + 1) * ff_block, :], preferred_element_type=F32)
        yield
    gate = jax.nn.sigmoid(
        jnp.dot(_rms(h, g_ple_ref[...]).astype(BF16), wg_ref[...], preferred_element_type=F32))
    h = h + gate * jnp.dot(p.astype(BF16), wple_ref[...], preferred_element_type=F32)
    return _rms(h, g_fin_ref[...])


def _out_ffn_kernel(x_ref, mix_ref, p_ref, *refs):
    *weight_refs, y_ref = refs
    _store_rows(y_ref, _run(_ffn_tile_steps(_load_rows(x_ref), _load_rows(mix_ref).astype(BF16),
                                            _load_rows(p_ref), *weight_refs)))


def _out_ffn(x, mix, p, wout, g_mlp, w1, w2, g_ple, wg, wple, g_fin, tm):
    fixed = lambda i: (0, 0)

    def resident(shape):
        return pl.BlockSpec(shape, fixed, pipeline_mode=pl.Buffered(1))

    return pl.pallas_call(
        _out_ffn_kernel,
        grid=(x.shape[0] // tm,),
        in_specs=[
            _row_spec(x.shape, tm, D_MODEL),
            _row_spec(x.shape, tm, 2 * D_MODEL),
            _row_spec(x.shape, tm, D_PLE),
            resident((2 * D_MODEL, D_MODEL)),
            resident((1, D_MODEL)),
            resident((D_MODEL, D_FF)),
            resident((D_FF, D_MODEL)),
            resident((1, D_MODEL)),
            resident((D_MODEL, D_MODEL)),
            resident((D_PLE, D_MODEL)),
            resident((1, D_MODEL)),
        ],
        out_specs=_row_spec(x.shape, tm, D_MODEL),
        out_shape=jax.ShapeDtypeStruct(x.shape, F32),
        compiler_params=pltpu.CompilerParams(
            dimension_semantics=("arbitrary",), vmem_limit_bytes=VMEM_LIMIT),
        name="out_ffn",
    )(x, mix, p, wout, g_mlp, w1, w2, g_ple, wg, wple, g_fin)


PROMPT_TILE = 256


def _mix_chunk_steps(u, z, xbc, dt_raw, pos0, keep, const_refs, pool_hist, conv_hist, state_t):
    (wpool_ref, pscale_ref, convw_ref, convb_ref, dtb_ref, alog_ref, dskip_ref, gssm_ref, tri_ref,
     expand_ref) = const_refs
    carried = (lambda v: v) if keep is None else (lambda v: v * keep)

    ext = jnp.concatenate([carried(pool_hist[...]), u], axis=0)
    pool_hist[...] = u[CHUNK - POOL_HIST:, :]
    pos = pos0 + lax.broadcasted_iota(jnp.int32, (CHUNK, 1), 0)
    y_pool = []
    for g, w in enumerate(POOL_WINDOWS):
        cols = slice(g * POOL_GROUP, (g + 1) * POOL_GROUP)
        s = ext[:, cols]
        shift = 1
        while shift < w:
            s = s + pltpu.roll(s, shift, axis=0)
            shift *= 2
        count = jnp.minimum(w, pos + 1).astype(F32)
        d = s[POOL_HIST:, :] / count - u[:, cols]
        y_pool.append(_dot(d, wpool_ref[g]))
    y_pool = jnp.concatenate(y_pool, axis=1) * pscale_ref[...]
    yield

    extc = jnp.concatenate([carried(conv_hist[...]), xbc], axis=0)
    conv_hist[...] = xbc[CHUNK - CONV_HIST:, :]
    acc = convb_ref[...]
    for k in range(CONV_WIDTH):
        back = CONV_WIDTH - 1 - k
        tap = extc if back == 0 else pltpu.roll(extc, back, axis=0)
        acc = acc + tap[CONV_HIST:, :] * convw_ref[k:k + 1, :]
    xc = _silu(acc)
    xs = xc[:, 0:D_SSM]
    b_all = xc[:, D_SSM:D_SSM + SSM_GROUPS * D_STATE]
    c_all = xc[:, D_SSM + SSM_GROUPS * D_STATE:CONV_DIM]
    yield

    dt = _softplus(dt_raw + dtb_ref[...])
    a_cum = _chunk_cumsum(tri_ref[...], dt * _neg_exp_heads(alog_ref[...]))
    a_last = a_cum[CHUNK - 1:CHUNK, :]
    expand = expand_ref[...]
    dt_t = dt.T
    a_log2 = a_cum * LOG2_E
    a_log2_t = a_log2.T
    ea_x = _expand_heads(jnp.exp(a_cum), expand)
    cd_x = _expand_heads(jnp.broadcast_to(jnp.exp(a_last), (2 * SUBLANES, DT_PAD)), expand)[0:1, :]
    x_w = xs * _expand_heads(jnp.exp(a_last - a_cum) * dt, expand)
    yield

    row_i = lax.broadcasted_iota(jnp.int32, (CHUNK, CHUNK), 0)
    col_i = lax.broadcasted_iota(jnp.int32, (CHUNK, CHUNK), 1)
    causal = row_i >= col_i
    lane = lax.broadcasted_iota(jnp.int32, (CHUNK, 2 * HEAD_DIM), 1)
    first_head = lane < HEAD_DIM

    y_parts = []
    for g in range(SSM_GROUPS):
        gch = slice(g * GROUP_WIDTH, (g + 1) * GROUP_WIDTH)
        b_g = b_all[:, g * D_STATE:(g + 1) * D_STATE].astype(BF16)
        c_g = c_all[:, g * D_STATE:(g + 1) * D_STATE].astype(BF16)
        cb = lax.dot_general(c_g, b_g, (((1,), (1,)), ((), ())), preferred_element_type=F32).astype(BF16)
        st_g = carried(state_t[:, gch])
        y_off = jnp.dot(c_g, st_g.astype(BF16), preferred_element_type=F32) * ea_x[:, gch]
        y_diag = []
        for j in range(HEADS_PER_GROUP // 2):
            h0 = g * HEADS_PER_GROUP + 2 * j
            m = []
            for h in (h0, h0 + 1):
                seg = a_log2[:, h:h + 1] - a_log2_t[h:h + 1, :]
                decay = jnp.exp2(jnp.where(causal, seg, -jnp.inf)).astype(BF16)
                dt_cols = jnp.broadcast_to(dt_t[h:h + 1, :], (CHUNK, CHUNK)).astype(BF16)
                m.append(decay * cb * dt_cols)
            pair = xs[:, h0 * HEAD_DIM:(h0 + 2) * HEAD_DIM]
            rhs = jnp.concatenate([jnp.where(first_head, pair, 0.0),
                                   jnp.where(first_head, 0.0, pair)], axis=0).astype(BF16)
            y_diag.append(jnp.dot(jnp.concatenate(m, axis=1), rhs, preferred_element_type=F32))
            yield
        y_parts.append(jnp.concatenate(y_diag, axis=1) + y_off)
        upd = lax.dot_general(b_g, x_w[:, gch].astype(BF16), (((0,), (0,)), ((), ())),
                              preferred_element_type=F32)
        state_t[:, gch] = st_g * cd_x[:, gch] + upd

    y = jnp.concatenate(y_parts, axis=1) + dskip_ref[...] * xs
    y = y * _silu(z)
    return y_pool, _rms(y, gssm_ref[...])


N_MIX_CONSTS = 10
N_FFN_WEIGHTS = 8
MIXER_STAGES_PER_FFN_STAGE = 4


def _prompt_mix_ffn_kernel(u_ref, z_ref, xbc_ref, dt_ref, x_ref, p_ref, *refs, n_tiles, tiles_per_seq):
    const_refs = refs[:N_MIX_CONSTS]
    weight_refs = refs[N_MIX_CONSTS:N_MIX_CONSTS + N_FFN_WEIGHTS]
    y_ref, ssm_ref, mix_scr, pool_hist, conv_hist, state_t = refs[N_MIX_CONSTS + N_FFN_WEIGHTS:]
    s = pl.program_id(0)

    @pl.when(s == 0)
    def _():
        mix_scr[...] = jnp.zeros_like(mix_scr)
        pool_hist[...] = jnp.zeros_like(pool_hist)
        conv_hist[...] = jnp.zeros_like(conv_hist)
        state_t[...] = jnp.zeros_like(state_t)

    tile_in_seq = jnp.minimum(s, n_tiles - 1) % tiles_per_seq
    keep = jnp.where(tile_in_seq == 0, 0.0, 1.0)
    slot = s % 2

    def mixer_steps():
        for ci in range(PROMPT_TILE // CHUNK):
            rows = slice(ci * CHUNK, (ci + 1) * CHUNK)
            y_pool, y_ssm = yield from _mix_chunk_steps(
                u_ref[rows, :], z_ref[rows, :], xbc_ref[rows, :], dt_ref[rows, :],
                tile_in_seq * PROMPT_TILE + ci * CHUNK, keep if ci == 0 else None,
                const_refs, pool_hist, conv_hist, state_t)
            mix_scr[slot, rows, 0:D_POOL] = y_pool.astype(BF16)
            mix_scr[slot, rows, D_POOL:] = y_ssm.astype(BF16)
            yield

    ffn_steps = _ffn_tile_steps(x_ref[...], mix_scr[(s + 1) % 2], p_ref[...], *weight_refs)
    y_ref[...] = _interleave(ffn_steps, mixer_steps(), MIXER_STAGES_PER_FFN_STAGE)

    @pl.when((tile_in_seq == tiles_per_seq - 1) & (s < n_tiles))
    def _():
        for j in range(D_SSM // LANES):
            ssm_ref[j * LANES:(j + 1) * LANES, :] = state_t[:, j * LANES:(j + 1) * LANES].T


def _prompt_mix_ffn(u, z, xbc, dt, x2d, p2d, consts, ffn_weights, batch, seq):
    tiles_per_seq = seq // PROMPT_TILE
    n_tiles = batch * tiles_per_seq
    mix_row = lambda s: (jnp.minimum(s, n_tiles - 1), 0)
    ffn_row = lambda s: (jnp.maximum(s - 1, 0), 0)
    fixed2 = lambda s: (0, 0)
    fixed3 = lambda s: (0, 0, 0)

    def resident(a):
        return pl.BlockSpec(a.shape, fixed3 if a.ndim == 3 else fixed2, pipeline_mode=pl.Buffered(1))

    return pl.pallas_call(
        functools.partial(_prompt_mix_ffn_kernel, n_tiles=n_tiles, tiles_per_seq=tiles_per_seq),
        grid=(n_tiles + 1,),
        in_specs=[
            pl.BlockSpec((PROMPT_TILE, D_POOL), mix_row),
            pl.BlockSpec((PROMPT_TILE, D_SSM), mix_row),
            pl.BlockSpec((PROMPT_TILE, CONV_DIM), mix_row),
            pl.BlockSpec((PROMPT_TILE, DT_PAD), mix_row),
            pl.BlockSpec((PROMPT_TILE, D_MODEL), ffn_row),
            pl.BlockSpec((PROMPT_TILE, D_PLE), ffn_row),
        ] + [resident(a) for a in consts] + [resident(a) for a in ffn_weights],
        out_specs=[
            pl.BlockSpec((PROMPT_TILE, D_MODEL), ffn_row),
            pl.BlockSpec((None, D_SSM, D_STATE),
                         lambda s: (jnp.minimum(s, n_tiles - 1) // tiles_per_seq, 0, 0)),
        ],
        out_shape=[
            jax.ShapeDtypeStruct((batch * seq, D_MODEL), F32),
            jax.ShapeDtypeStruct((batch, D_SSM, D_STATE), F32),
        ],
        scratch_shapes=[
            pltpu.VMEM((2, PROMPT_TILE, 2 * D_MODEL), BF16),
            pltpu.VMEM((POOL_HIST, D_POOL), F32),
            pltpu.VMEM((CONV_HIST, CONV_DIM), F32),
            pltpu.VMEM((D_STATE, D_SSM), F32),
        ],
        compiler_params=pltpu.CompilerParams(
            dimension_semantics=("arbitrary",), vmem_limit_bytes=VMEM_LIMIT),
        name="prompt_mix_ffn",
    )(u, z, xbc, dt, x2d, p2d, *consts, *ffn_weights)


SAMPLE_T = 4
SAMPLE_BB = 16


def _sample_mix_kernel(u_ref, z_ref, xbc_ref, dt_ref, pbuf_ref, cbuf_ref, state_ref,
                       wpool_ref, pscale_ref, convw_ref, convb_ref, dtb_ref, alog_ref, dskip_ref,
                       gssm_ref, expand_ref,
                       mix_ref, npool_ref, nconv_ref, nstate_ref):
    nt, bb = SAMPLE_T, SAMPLE_BB
    rows = nt * bb

    at_time = lambda ref, t: ref[:, t, :]
    ext = [pbuf_ref[i] for i in range(POOL_BUF)] + [at_time(u_ref, t) for t in range(nt)]
    for i in range(POOL_BUF):
        npool_ref[i] = ext[i + nt]
    y_pool = []
    for g, w in enumerate(POOL_WINDOWS):
        cols = slice(g * POOL_GROUP, (g + 1) * POOL_GROUP)
        d = []
        for t in range(nt):
            hi = POOL_BUF + t
            win = ext[hi - w + 1][:, cols]
            for i in range(hi - w + 2, hi + 1):
                win = win + ext[i][:, cols]
            count = float(min(w, PAST_LEN + t + 1))
            d.append(win / count - ext[hi][:, cols])
        y_pool.append(_dot(jnp.concatenate(d, axis=0), wpool_ref[g]))
    y_pool = jnp.concatenate(y_pool, axis=1) * pscale_ref[...]

    extc = [cbuf_ref[i] for i in range(CONV_WIDTH - 1)] + [at_time(xbc_ref, t) for t in range(nt)]
    for j in range(CONV_WIDTH - 1):
        nconv_ref[j] = extc[nt + j]
    xc = []
    for t in range(nt):
        acc = convb_ref[...]
        for k in range(CONV_WIDTH):
            acc = acc + extc[t + k] * convw_ref[k:k + 1, :]
        xc.append(_silu(acc))
    xs = [v[:, 0:D_SSM] for v in xc]
    bm = [v[:, D_SSM:D_SSM + SSM_GROUPS * D_STATE] for v in xc]
    cm = [v[:, D_SSM + SSM_GROUPS * D_STATE:CONV_DIM] for v in xc]

    a_row = _neg_exp_heads(alog_ref[...])
    dt = [_softplus(at_time(dt_ref, t) + dtb_ref[...]) for t in range(nt)]
    a_cum = []
    for t in range(nt):
        a_t = dt[t] * a_row
        a_cum.append(a_t if t == 0 else a_cum[-1] + a_t)
    lane = lax.broadcasted_iota(jnp.int32, (bb, DT_PAD), 1)
    in_group0 = lane < HEADS_PER_GROUP
    head_rows = []
    pair_index = {}
    for l in range(nt):
        for s in range(l + 1):
            cb = [jnp.sum(cm[l][:, g * D_STATE:(g + 1) * D_STATE] * bm[s][:, g * D_STATE:(g + 1) * D_STATE],
                          axis=-1, keepdims=True) for g in range(SSM_GROUPS)]
            cb_heads = jnp.where(in_group0, cb[0], cb[1])
            pair_index[(l, s)] = len(head_rows)
            head_rows.append(cb_heads * jnp.exp(a_cum[l] - a_cum[s]) * dt[s])
    ea_index = len(head_rows)
    head_rows += [jnp.exp(a_cum[t]) for t in range(nt)]
    w_index = len(head_rows)
    head_rows += [jnp.exp(a_cum[nt - 1] - a_cum[t]) * dt[t] for t in range(nt)]
    cd_index = len(head_rows)
    head_rows.append(jnp.exp(a_cum[nt - 1]))
    chan = _expand_heads(jnp.concatenate(head_rows, axis=0), expand_ref[...])
    chan_rows = lambda i: chan[i * bb:(i + 1) * bb, :]

    y_diag = []
    for l in range(nt):
        acc = chan_rows(pair_index[(l, 0)]) * xs[0]
        for s in range(1, l + 1):
            acc = acc + chan_rows(pair_index[(l, s)]) * xs[s]
        y_diag.append(acc)
    xs_all = jnp.concatenate(xs, axis=0)
    xw_all = jnp.concatenate([chan_rows(w_index + t) * xs[t] for t in range(nt)], axis=0)
    ea_all = jnp.concatenate([chan_rows(ea_index + t) for t in range(nt)], axis=0)
    b_all = jnp.concatenate(bm, axis=0).astype(BF16)
    c_all = jnp.concatenate(cm, axis=0).astype(BF16)
    cd3 = jnp.concatenate([p.astype(F32) for p in _split3(chan_rows(cd_index))], axis=0)
    ones3 = jnp.ones((3 * bb, D_STATE), BF16)
    row_batch = lax.broadcasted_iota(jnp.int32, (rows, 1), 0) & (bb - 1)
    row_batch3 = lax.broadcasted_iota(jnp.int32, (3 * bb, 1), 0) & (bb - 1)
    contract_rows = (((0,), (0,)), ((), ()))

    def per_batch(b, y_off):
        st = state_ref[b]
        st16 = st.astype(BF16)
        mine = row_batch == b
        xw_b = jnp.where(mine, xw_all, 0.0).astype(BF16)
        cd_b = jnp.where(row_batch3 == b, cd3, 0.0).astype(BF16)
        scale = lax.dot_general(cd_b, ones3, contract_rows, preferred_element_type=F32)
        yo = []
        for g in range(SSM_GROUPS):
            grow = slice(g * GROUP_WIDTH, (g + 1) * GROUP_WIDTH)
            gst = slice(g * D_STATE, (g + 1) * D_STATE)
            yo.append(lax.dot_general(c_all[:, gst], st16[grow, :], (((1,), (1,)), ((), ())),
                                      preferred_element_type=F32))
            upd = lax.dot_general(xw_b[:, grow], b_all[:, gst], contract_rows,
                                  preferred_element_type=F32)
            nstate_ref[b, grow, :] = st[grow, :] * scale[grow, :] + upd
        return jnp.where(mine, jnp.concatenate(yo, axis=1), y_off)

    y_off = lax.fori_loop(0, bb, per_batch, jnp.zeros((rows, D_SSM), F32), unroll=True)
    y = jnp.concatenate(y_diag, axis=0) + y_off * ea_all + dskip_ref[...] * xs_all
    y = y * _silu(jnp.concatenate([at_time(z_ref, t) for t in range(nt)], axis=0))
    y_ssm = _rms(y, gssm_ref[...])
    for t in range(nt):
        mix_ref[:, t, 0:D_POOL] = y_pool[t * bb:(t + 1) * bb, :].astype(mix_ref.dtype)
        mix_ref[:, t, D_POOL:] = y_ssm[t * bb:(t + 1) * bb, :].astype(mix_ref.dtype)


def _sample_group_mix(u, z, xbc, dt, pbuf, cbuf, state, consts):
    nt, bb = SAMPLE_T, SAMPLE_BB
    batch = state.shape[0]
    tblk = lambda i: (0, i, 0)
    (wpool, pscale, convw, convb, dtb, alog, dskip_x, gssm, _, expand) = consts
    blk3 = lambda i: (i, 0, 0)
    fixed2 = lambda i: (0, 0)
    fixed3 = lambda i: (0, 0, 0)
    return pl.pallas_call(
        _sample_mix_kernel,
        grid=(batch // bb,),
        in_specs=[
            pl.BlockSpec((bb, nt, D_POOL), blk3),
            pl.BlockSpec((bb, nt, D_SSM), blk3),
            pl.BlockSpec((bb, nt, CONV_DIM), blk3),
            pl.BlockSpec((bb, nt, DT_PAD), blk3),
            pl.BlockSpec((POOL_BUF, bb, D_POOL), tblk),
            pl.BlockSpec((CONV_WIDTH - 1, bb, CONV_DIM), tblk),
            pl.BlockSpec((bb, D_SSM, D_STATE), blk3),
            pl.BlockSpec((len(POOL_WINDOWS), POOL_GROUP, POOL_GROUP), fixed3),
            pl.BlockSpec((1, D_POOL), fixed2),
            pl.BlockSpec((CONV_WIDTH, CONV_DIM), fixed2),
            pl.BlockSpec((1, CONV_DIM), fixed2),
            pl.BlockSpec((1, DT_PAD), fixed2),
            pl.BlockSpec((1, DT_PAD), fixed2),
            pl.BlockSpec((1, D_SSM), fixed2),
            pl.BlockSpec((1, D_SSM), fixed2),
            pl.BlockSpec((DT_PAD, D_SSM), fixed2),
        ],
        out_specs=[
            pl.BlockSpec((bb, nt, 2 * D_MODEL), blk3),
            pl.BlockSpec((POOL_BUF, bb, D_POOL), tblk),
            pl.BlockSpec((CONV_WIDTH - 1, bb, CONV_DIM), tblk),
            pl.BlockSpec((bb, D_SSM, D_STATE), blk3),
        ],
        out_shape=[
            jax.ShapeDtypeStruct((batch, nt, 2 * D_MODEL), F32),
            jax.ShapeDtypeStruct((POOL_BUF, batch, D_POOL), F32),
            jax.ShapeDtypeStruct((CONV_WIDTH - 1, batch, CONV_DIM), F32),
            jax.ShapeDtypeStruct((batch, D_SSM, D_STATE), F32),
        ],
        compiler_params=pltpu.CompilerParams(
            dimension_semantics=("arbitrary",), vmem_limit_bytes=VMEM_LIMIT),
        name="sample_mix",
    )(u, z, xbc, dt, pbuf, cbuf, state, wpool, pscale, convw, convb, dtb, alog, dskip_x, gssm, expand)


def _mix_constants(w_pool, pool_scale, conv_w, conv_b, dt_bias, a_log, d_skip, ssm_norm_g):
    pad = DT_PAD - SSM_HEADS
    head_of_channel = jnp.arange(D_SSM, dtype=jnp.int32) // HEAD_DIM
    slot_row = jnp.arange(DT_PAD, dtype=jnp.int32)
    expand = ((slot_row[:, None] % HEAD_SLOT == head_of_channel[None, :])
              & (slot_row[:, None] < 3 * HEAD_SLOT)).astype(BF16)
    tri = (jnp.arange(CHUNK)[:, None] >= jnp.arange(CHUNK)[None, :]).astype(BF16)
    return (
        w_pool.astype(BF16),
        pool_scale.reshape(1, D_POOL),
        conv_w,
        conv_b.reshape(1, CONV_DIM),
        jnp.pad(dt_bias, (0, pad)).reshape(1, DT_PAD),
        jnp.pad(a_log, (0, pad)).reshape(1, DT_PAD),
        jnp.repeat(d_skip, HEAD_DIM).reshape(1, D_SSM),
        ssm_norm_g.reshape(1, D_SSM),
        tri,
        expand,
    )


PROMPT_IN_TM = 512
SAMPLE_TB = 64


def kernel(x_prompt, x_sample, p_prompt, p_sample, state_pool, state_conv, state_ssm, w_in, w_pool, pool_scale, conv_w, conv_b, dt_bias, a_log, d_skip, ssm_norm_g, w_out, norm_mix_g, norm_mlp_g, w_ff1, w_ff2, norm_ple_g, w_gate, w_ple, final_norm_g):
    assert w_in.shape[0] == 1, "single-layer trunk"
    bp, tp = x_prompt.shape[:2]
    bs, ts = x_sample.shape[:2]
    assert tp % PROMPT_TILE == 0 and tp >= POOL_BUF and ts == SAMPLE_T and bs % SAMPLE_BB == 0

    row = lambda v: v.reshape(1, -1)
    w_in_t = w_in[0].T
    consts = _mix_constants(w_pool[0], pool_scale[0], conv_w[0], conv_b[0], dt_bias[0], a_log[0],
                            d_skip[0], ssm_norm_g[0])
    g_mix = row(norm_mix_g[0])

    xp = x_prompt.reshape(bp * tp, D_MODEL)
    u, z, xbc, dt, w_in_t16, w_out16, w_ff1_16, w_ff2_16, w_gate16 = _in_proj(
        xp, g_mix, w_in_t, PROMPT_IN_TM, ride_along=(w_out[0], w_ff1[0], w_ff2[0], w_gate[0]))
    ffn_weights = (w_out16, row(norm_mlp_g[0]), w_ff1_16, w_ff2_16, row(norm_ple_g[0]), w_gate16,
                   w_ple[0].astype(BF16), row(final_norm_g))
    y_prompt, ssm_p = _prompt_mix_ffn(u, z, xbc, dt, xp, p_prompt[0].reshape(bp * tp, D_PLE), consts,
                                      ffn_weights, bp, tp)
    pool_p = u.reshape(bp, tp, D_POOL)[:, tp - POOL_BUF:, :]
    conv_p = xbc.reshape(bp, tp, CONV_DIM)[:, tp - (CONV_WIDTH - 1):, :]

    tmaj = lambda a: jnp.transpose(a, (1, 0, 2))
    u_s, z_s, xbc_s, dt_s = _in_proj(x_sample, g_mix, w_in_t16, SAMPLE_TB)
    mix_s, pool_s, conv_s, ssm_s = _sample_group_mix(
        u_s, z_s, xbc_s, dt_s, tmaj(state_pool[0]), tmaj(state_conv[0]),
        state_ssm[0].reshape(bs, D_SSM, D_STATE), consts)
    y_sample = _out_ffn(x_sample, mix_s, p_sample[0], *ffn_weights, SAMPLE_TB)

    state_shape = (1, -1, SSM_HEADS, HEAD_DIM, D_STATE)
    return (y_prompt.reshape(bp, tp, D_MODEL), y_sample,
            pool_p[None], conv_p[None], ssm_p.reshape(state_shape),
            tmaj(pool_s)[None], tmaj(conv_s)[None], ssm_s.reshape(state_shape))
```
